```python
import math
import jax, jax.numpy as jnp
from jax import lax
import numpy as np

D_MODEL = 4096
BATCH = 4
SEQ = 4096
DEPTH = 4
DEC_BATCH = 1
DEC_SEQ = 16384
PAST_LEN = 128

HEAD_DIM = 128
MIX_WIDTH = D_MODEL
FOURIER_WIDTH = D_MODEL // 4
FOURIER_GROUPS = FOURIER_WIDTH // HEAD_DIM
ATTN_WIDTH = MIX_WIDTH - FOURIER_WIDTH
N_HEADS = ATTN_WIDTH // HEAD_DIM
KV_RATIO = 4
N_KV_HEADS = N_HEADS // KV_RATIO
KV_WIDTH = N_KV_HEADS * HEAD_DIM
WINDOW = 128
BLOCK = 128
AB_IN_WIDTH = ATTN_WIDTH + 2 * KV_WIDTH + FOURIER_WIDTH
CONV_WIDTH = MIX_WIDTH // 2
CONV_KERNEL = 31
HYENA_WIDTH = MIX_WIDTH - CONV_WIDTH
HYENA_SHORT = 3
HYENA_FILTER_HIDDEN = 64
HYENA_BANDS = 16
HYENA_EMB = 1 + 2 * HYENA_BANDS
HYENA_MIN_DECAY = math.log(1e-2) / 1.5
HYENA_MAX_DECAY = math.log(1e-2) / 0.3
CD_IN_WIDTH = 2 * CONV_WIDTH + 3 * HYENA_WIDTH
D_FF = -(-(8 * D_MODEL) // (3 * 256)) * 256
N_EVEN = (DEPTH + 1) // 2
N_ODD = DEPTH // 2
EPS = 1e-6
NEG_INF = -1e30

kernel_name = 'hybrid_swa_fnet_conformer_hyena_encoder'


def rms_norm(x, g):
    xf = x.astype(jnp.float32)
    y = xf * lax.rsqrt(jnp.mean(xf * xf, axis=-1, keepdims=True) + EPS)
    return (y * g.astype(jnp.float32)).astype(x.dtype)


def layer_norm(x, g, b):
    xf = x.astype(jnp.float32)
    mu = jnp.mean(xf, axis=-1, keepdims=True)
    xc = xf - mu
    var = jnp.mean(xc * xc, axis=-1, keepdims=True)
    return (xc * lax.rsqrt(var + EPS) * g.astype(jnp.float32) + b.astype(jnp.float32)).astype(x.dtype)


def alibi_slopes(n):
    def pow2(m):
        start = 2.0 ** (-(2.0 ** -(math.log2(m) - 3)))
        return [start ** (i + 1) for i in range(m)]
    if math.log2(n).is_integer():
        s = pow2(n)
    else:
        c = 2 ** math.floor(math.log2(n))
        s = pow2(c) + pow2(2 * c)[0::2][: n - c]
    return jnp.asarray(np.asarray(s, np.float32))


def depthwise_conv(x, w, b):
    k = w.shape[0]
    y = lax.conv_general_dilated(
        x, w[:, None, :].astype(x.dtype), window_strides=(1,),
        padding=[(k // 2, k // 2)], dimension_numbers=('NWC', 'WIO', 'NWC'),
        feature_group_count=x.shape[-1])
    return y + b.astype(x.dtype)


def banded_gqa_sink(q, k, v, sink, slopes):
    b, l = q.shape[0], q.shape[1]
    nb = l // BLOCK
    qb = q.reshape(b, nb, BLOCK, N_KV_HEADS, KV_RATIO, HEAD_DIM)

    def bands(t):
        tp = jnp.pad(t, ((0, 0), (BLOCK, BLOCK), (0, 0), (0, 0)))
        tp = tp.reshape(b, nb + 2, BLOCK, N_KV_HEADS, HEAD_DIM)
        return jnp.concatenate([tp[:, :-2], tp[:, 1:-1], tp[:, 2:]], axis=2)

    kb, vb = bands(k), bands(v)
    s = jnp.einsum('bnqgrd,bnkgd->bngrqk', qb, kb,
                   preferred_element_type=jnp.float32) * (HEAD_DIM ** -0.5)
    k_off = jnp.arange(3 * BLOCK) - BLOCK
    rel = k_off[None, :] - jnp.arange(BLOCK)[:, None]
    dist = jnp.abs(rel)
    key_pos = jnp.arange(nb)[:, None] * BLOCK + k_off[None, :]
    valid = (dist <= WINDOW)[None] & ((key_pos >= 0) & (key_pos < l))[:, None, :]
    alibi = -slopes.reshape(N_KV_HEADS, KV_RATIO)[:, :, None, None] * dist.astype(jnp.float32)
    s = jnp.where(valid[None, :, None, None], s + alibi[None, None], NEG_INF)
    sink_col = jnp.broadcast_to(
        sink.astype(jnp.float32).reshape(N_KV_HEADS, KV_RATIO)[None, None, :, :, None, None],
        s.shape[:-1] + (1,))
    p = jax.nn.softmax(jnp.concatenate([s, sink_col], axis=-1), axis=-1)[..., :-1]
    o = jnp.einsum('bngrqk,bnkgd->bnqgrd', p.astype(v.dtype), vb)
    return o.reshape(b, l, N_HEADS * HEAD_DIM)


def fourier_mix(u):
    b, l = u.shape[0], u.shape[1]
    ug = u.reshape(b, l, FOURIER_GROUPS, HEAD_DIM).astype(jnp.float32)
    y = jnp.fft.fft2(ug, axes=(1, 3), norm='ortho').real
    return y.reshape(b, l, FOURIER_WIDTH).astype(u.dtype)


def hyena_kernel(l, w1, b1, w2, b2, w3, b3, freq, w4):
    f32 = jnp.float32
    t = jnp.linspace(0.0, 1.0, l, dtype=f32)[:, None]
    t_res = jnp.arange(l, dtype=f32)[:, None]
    bands = jnp.linspace(1e-4, HYENA_BANDS - 1, HYENA_BANDS, dtype=f32)[None, :]
    ang = (2.0 * math.pi / l) * bands * t_res
    z = jnp.concatenate([t, jnp.cos(ang), -jnp.sin(ang)], axis=-1)
    fr = freq.astype(f32)
    h = jnp.sin(fr * (z @ w1.astype(f32) + b1.astype(f32)))
    h = jnp.sin(fr * (h @ w2.astype(f32) + b2.astype(f32)))
    h = jnp.sin(fr * (h @ w3.astype(f32) + b3.astype(f32)))
    h = (h @ w4.astype(f32)).reshape(l, 2, HYENA_WIDTH)
    deltas = jnp.abs(jnp.linspace(HYENA_MIN_DECAY, HYENA_MAX_DECAY, HYENA_WIDTH, dtype=f32))
    h = h * jnp.exp(-t * deltas[None, :])[:, None, :]
    kern = jnp.concatenate([h[:, 0], jnp.zeros((1, HYENA_WIDTH), f32), h[:0:-1, 1]], axis=0)
    return kern / jnp.sum(jnp.abs(kern), axis=0, keepdims=True)


def long_conv(u, kern):
    l = u.shape[1]
    uf = jnp.fft.rfft(u.astype(jnp.float32), n=2 * l, axis=1)
    kf = jnp.fft.rfft(kern, axis=0)
    return jnp.fft.irfft(uf * kf[None], n=2 * l, axis=1)[:, :l]


def mixer_ab(h, w_in, w_out, sink, slopes):
    b, l = h.shape[0], h.shape[1]
    z = h @ w_in
    q, k, v, u = jnp.split(z, [ATTN_WIDTH, ATTN_WIDTH + KV_WIDTH, ATTN_WIDTH + 2 * KV_WIDTH], axis=-1)
    a = banded_gqa_sink(q.reshape(b, l, N_HEADS, HEAD_DIM),
                        k.reshape(b, l, N_KV_HEADS, HEAD_DIM),
                        v.reshape(b, l, N_KV_HEADS, HEAD_DIM), sink, slopes)
    f = fourier_mix(u)
    return jnp.concatenate([a, f], axis=-1) @ w_out


def mixer_cd(h, w_in, w_out, dw_w, dw_b, ln_g, ln_b, sw, sb, w1, b1, w2, b2, w3, b3, freq, w4, hy_bias):
    l = h.shape[1]
    z = h @ w_in
    zc, zh = z[..., : 2 * CONV_WIDTH], z[..., 2 * CONV_WIDTH:]
    ca, cg = jnp.split(zc, 2, axis=-1)
    c = ca * jax.nn.sigmoid(cg)
    c = jax.nn.silu(layer_norm(depthwise_conv(c, dw_w, dw_b), ln_g, ln_b))
    zh = depthwise_conv(zh, sw, sb)
    x0, x1, v = jnp.split(zh, 3, axis=-1)
    v = v * x1
    kern = hyena_kernel(l, w1, b1, w2, b2, w3, b3, freq, w4)
    y = (long_conv(v, kern) + v.astype(jnp.float32) * hy_bias.astype(jnp.float32)).astype(h.dtype)
    d = x0 * y
    return jnp.concatenate([c, d], axis=-1) @ w_out


def swiglu(h, wg, wu, wd):
    return (jax.nn.silu(h @ wg) * (h @ wu)) @ wd


def trunk(x, p):
    slopes = alibi_slopes(N_HEADS)
    for layer in range(DEPTH):
        h = rms_norm(x, p['mix_norm'][layer])
        if layer % 2 == 0:
            i = layer // 2
            x = x + mixer_ab(h, p['ab_w_in'][i], p['ab_w_out'][i], p['attn_sink'][i], slopes)
        else:
            i = layer // 2
            x = x + mixer_cd(h, p['cd_w_in'][i], p['cd_w_out'][i], p['conv_dw_w'][i], p['conv_dw_b'][i],
                             p['conv_ln_g'][i], p['conv_ln_b'][i], p['hy_short_w'][i], p['hy_short_b'][i],
                             p['hy_filt_w1'][i], p['hy_filt_b1'][i], p['hy_filt_w2'][i], p['hy_filt_b2'][i],
                             p['hy_filt_w3'][i], p['hy_filt_b3'][i], p['hy_filt_freq'][i], p['hy_filt_w4'][i],
                             p['hy_bias'][i])
        x = x + swiglu(rms_norm(x, p['ffn_norm'][layer]), p['w_gate'][layer], p['w_up'][layer], p['w_down'][layer])
    return rms_norm(x, p['final_norm'])


def setup_inputs(seed: int = 0) -> dict:
    key = jax.random.key(seed)
    ks = jax.random.split(key, 32)
    f32 = jnp.float32

    def nrm(k, shape, scale):
        return jax.random.normal(k, shape, f32) * scale

    return {
        'x_prompt': nrm(ks[0], (BATCH, SEQ, D_MODEL), 1.0),
        'x_sample': nrm(ks[1], (DEC_BATCH, DEC_SEQ, D_MODEL), 1.0),
        'mix_norm': 1.0 + nrm(ks[2], (DEPTH, D_MODEL), 0.02),
        'ffn_norm': 1.0 + nrm(ks[3], (DEPTH, D_MODEL), 0.02),
        'final_norm': 1.0 + nrm(ks[4], (D_MODEL,), 0.02),
        'w_gate': nrm(ks[5], (DEPTH, D_MODEL, D_FF), D_MODEL ** -0.5),
        'w_up': nrm(ks[6], (DEPTH, D_MODEL, D_FF), D_MODEL ** -0.5),
        'w_down': nrm(ks[7], (DEPTH, D_FF, D_MODEL), D_FF ** -0.5),
        'ab_w_in': nrm(ks[8], (N_EVEN, D_MODEL, AB_IN_WIDTH), D_MODEL ** -0.5),
        'ab_w_out': nrm(ks[9], (N_EVEN, MIX_WIDTH, D_MODEL), MIX_WIDTH ** -0.5),
        'attn_sink': nrm(ks[10], (N_EVEN, N_HEADS), 0.5),
        'cd_w_in': nrm(ks[11], (N_ODD, D_MODEL, CD_IN_WIDTH), D_MODEL ** -0.5),
        'cd_w_out': nrm(ks[12], (N_ODD, MIX_WIDTH, D_MODEL), MIX_WIDTH ** -0.5),
        'conv_dw_w': nrm(ks[13], (N_ODD, CONV_KERNEL, CONV_WIDTH), CONV_KERNEL ** -0.5),
        'conv_dw_b': nrm(ks[14], (N_ODD, CONV_WIDTH), 0.02),
        'conv_ln_g': 1.0 + nrm(ks[15], (N_ODD, CONV_WIDTH), 0.02),
        'conv_ln_b': nrm(ks[16], (N_ODD, CONV_WIDTH), 0.02),
        'hy_short_w': nrm(ks[17], (N_ODD, HYENA_SHORT, 3 * HYENA_WIDTH), HYENA_SHORT ** -0.5),
        'hy_short_b': nrm(ks[18], (N_ODD, 3 * HYENA_WIDTH), 0.02),
        'hy_filt_w1': nrm(ks[19], (N_ODD, HYENA_EMB, HYENA_FILTER_HIDDEN), HYENA_EMB ** -0.5),
        'hy_filt_b1': nrm(ks[20], (N_ODD, HYENA_FILTER_HIDDEN), 0.02),
        'hy_filt_w2': nrm(ks[21], (N_ODD, HYENA_FILTER_HIDDEN, HYENA_FILTER_HIDDEN), HYENA_FILTER_HIDDEN ** -0.5),
        'hy_filt_b2': nrm(ks[22], (N_ODD, HYENA_FILTER_HIDDEN), 0.02),
        'hy_filt_w3': nrm(ks[23], (N_ODD, HYENA_FILTER_HIDDEN, HYENA_FILTER_HIDDEN), HYENA_FILTER_HIDDEN ** -0.5),
        'hy_filt_b3': nrm(ks[24], (N_ODD, HYENA_FILTER_HIDDEN), 0.02),
        'hy_filt_freq': 1.0 + nrm(ks[25], (N_ODD, HYENA_FILTER_HIDDEN), 0.1),
        'hy_filt_w4': nrm(ks[26], (N_ODD, HYENA_FILTER_HIDDEN, 2 * HYENA_WIDTH), HYENA_FILTER_HIDDEN ** -0.5),
        'hy_bias': nrm(ks[27], (N_ODD, HYENA_WIDTH), 1.0),
    }


def reference(x_prompt, x_sample, mix_norm, ffn_norm, final_norm, w_gate, w_up, w_down,
              ab_w_in, ab_w_out, attn_sink, cd_w_in, cd_w_out, conv_dw_w, conv_dw_b,
              conv_ln_g, conv_ln_b, hy_short_w, hy_short_b, hy_filt_w1, hy_filt_b1,
              hy_filt_w2, hy_filt_b2, hy_filt_w3, hy_filt_b3, hy_filt_freq, hy_filt_w4, hy_bias):
    params = dict(mix_norm=mix_norm, ffn_norm=ffn_norm, final_norm=final_norm,
                  w_gate=w_gate, w_up=w_up, w_down=w_down,
                  ab_w_in=ab_w_in, ab_w_out=ab_w_out, attn_sink=attn_sink,
                  cd_w_in=cd_w_in, cd_w_out=cd_w_out, conv_dw_w=conv_dw_w, conv_dw_b=conv_dw_b,
                  conv_ln_g=conv_ln_g, conv_ln_b=conv_ln_b, hy_short_w=hy_short_w, hy_short_b=hy_short_b,
                  hy_filt_w1=hy_filt_w1, hy_filt_b1=hy_filt_b1, hy_filt_w2=hy_filt_w2, hy_filt_b2=hy_filt_b2,
                  hy_filt_w3=hy_filt_w3, hy_filt_b3=hy_filt_b3, hy_filt_freq=hy_filt_freq,
                  hy_filt_w4=hy_filt_w4, hy_bias=hy_bias)
    y_prompt = trunk(x_prompt, params)
    y_sample = trunk(x_sample, params)
    return (y_prompt, y_sample)
```

```python
import functools
import math

import numpy as np
import jax
import jax.numpy as jnp
from jax import lax
from jax.experimental import pallas as pl
from jax.experimental.pallas import tpu as pltpu

F32 = jnp.float32
BF16 = jnp.bfloat16

HEAD_DIM = 128
KV_RATIO = 4
WINDOW = 128
BLOCK = 128
CONV_KERNEL = 31
CONV_HALO = 16
HYENA_SHORT = 3
HYENA_BANDS = 16
HYENA_MIN_DECAY = math.log(1e-2) / 1.5
HYENA_MAX_DECAY = math.log(1e-2) / 0.3
EPS = 1e-6
NEG_INF = -1e30

V7X_VMEM_BYTES = 64 * 1024 * 1024
VMEM_LIMIT = V7X_VMEM_BYTES - 8 * 1024 * 1024
LANES = 128
SUBLANES = 8


def _params(*sem):
    return pltpu.CompilerParams(dimension_semantics=sem, vmem_limit_bytes=VMEM_LIMIT)


def _pick(n, candidates):
    for c in candidates:
        if c <= n and n % c == 0:
            return c
    return n


def _rms_kernel(x_ref, g_ref, o_ref):
    x = x_ref[...]
    ms = jnp.mean(x * x, axis=-1, keepdims=True)
    o_ref[...] = (x * lax.rsqrt(ms + EPS) * g_ref[...]).astype(o_ref.dtype)


def rms_norm(x, g, out_dtype):
    m, d = x.shape
    tr = _pick(m, (256, 128, 64, 32, 16, 8))
    return pl.pallas_call(
        _rms_kernel,
        out_shape=jax.ShapeDtypeStruct((m, d), out_dtype),
        grid=(m // tr,),
        in_specs=[pl.BlockSpec((tr, d), lambda i: (i, 0)),
                  pl.BlockSpec((1, d), lambda i: (0, 0))],
        out_specs=pl.BlockSpec((tr, d), lambda i: (i, 0)),
        compiler_params=_params("parallel"),
        name="rms_norm",
    )(x, g.reshape(1, d))


def _proj_kernel(a_ref, w_ref, o_ref):
    o_ref[...] = jnp.dot(a_ref[...], w_ref[...], preferred_element_type=F32).astype(o_ref.dtype)


def proj(a, w, layer, out_dtype):
    m, k = a.shape
    n = w.shape[-1]
    tm = _pick(m, (1024, 512, 256, 128))
    tn = _pick(n, (1024, 512, 256, 128))
    return pl.pallas_call(
        _proj_kernel,
        out_shape=jax.ShapeDtypeStruct((m, n), out_dtype),
        grid=(m // tm, n // tn),
        in_specs=[pl.BlockSpec((tm, k), lambda i, j: (i, 0)),
                  pl.BlockSpec((None, k, tn), lambda i, j: (layer, 0, j))],
        out_specs=pl.BlockSpec((tm, tn), lambda i, j: (i, j)),
        compiler_params=_params("parallel", "arbitrary"),
        name="proj",
    )(a, w)


def _swiglu_kernel(a_ref, wg_ref, wu_ref, o_ref):
    a = a_ref[...]
    g = jnp.dot(a, wg_ref[...], preferred_element_type=F32)
    u = jnp.dot(a, wu_ref[...], preferred_element_type=F32)
    o_ref[...] = (g * jax.nn.sigmoid(g) * u).astype(o_ref.dtype)


def swiglu_in(a, wg, wu, layer):
    m, k = a.shape
    n = wg.shape[-1]
    tm = _pick(m, (1024, 512, 256, 128))
    tn = _pick(n, (512, 256, 128))
    return pl.pallas_call(
        _swiglu_kernel,
        out_shape=jax.ShapeDtypeStruct((m, n), BF16),
        grid=(m // tm, n // tn),
        in_specs=[pl.BlockSpec((tm, k), lambda i, j: (i, 0)),
                  pl.BlockSpec((None, k, tn), lambda i, j: (layer, 0, j)),
                  pl.BlockSpec((None, k, tn), lambda i, j: (layer, 0, j))],
        out_specs=pl.BlockSpec((tm, tn), lambda i, j: (i, j)),
        compiler_params=_params("parallel", "arbitrary"),
        name="swiglu_in",
    )(a, wg, wu)


def _out_kernel(*refs, n_parts):
    a_refs = refs[:n_parts]
    w_refs = refs[n_parts:2 * n_parts]
    res_ref = refs[2 * n_parts]
    o_ref = refs[2 * n_parts + 1]
    acc = res_ref[...]
    for a_ref, w_ref in zip(a_refs, w_refs):
        acc = acc + jnp.dot(a_ref[...], w_ref[...], preferred_element_type=F32)
    o_ref[...] = acc


def out_proj(parts, w, layer, res):
    m = res.shape[0]
    n = w.shape[-1]
    kp = [p.shape[1] for p in parts]
    ktot = max(kp)
    tm = _pick(m, (1024, 512, 256, 128)) if ktot <= 4096 else _pick(m, (512, 256, 128))
    tn = _pick(n, (512, 256, 128)) if ktot <= 4096 else _pick(n, (256, 128))
    in_specs = [pl.BlockSpec((tm, k), lambda i, j: (i, 0)) for k in kp]
    off = 0
    for k in kp:
        assert off % k == 0, "each part must start at a multiple of its own width"
        blk = off // k
        in_specs.append(pl.BlockSpec((None, k, tn), lambda i, j, blk=blk: (layer, blk, j)))
        off += k
    in_specs.append(pl.BlockSpec((tm, tn), lambda i, j: (i, j)))
    return pl.pallas_call(
        functools.partial(_out_kernel, n_parts=len(parts)),
        out_shape=jax.ShapeDtypeStruct((m, n), F32),
        grid=(m // tm, n // tn),
        in_specs=in_specs,
        out_specs=pl.BlockSpec((tm, tn), lambda i, j: (i, j)),
        compiler_params=_params("parallel", "arbitrary"),
        name="out_proj",
    )(*parts, *([w] * len(parts)), res)


def alibi_slopes(n):
    def pow2(m):
        start = 2.0 ** (-(2.0 ** -(math.log2(m) - 3)))
        return [start ** (i + 1) for i in range(m)]
    if math.log2(n).is_integer():
        s = pow2(n)
    else:
        c = 2 ** math.floor(math.log2(n))
        s = pow2(c) + pow2(2 * c)[0::2][: n - c]
    return np.asarray(s, np.float32)


def _attn_kernel(sink_ref, slope_ref, q_ref, kp_ref, kc_ref, kn_ref, vp_ref, vc_ref, vn_ref, o_ref):
    n = pl.program_id(1)
    g = pl.program_id(2)
    nb = pl.num_programs(1)
    k = jnp.concatenate([kp_ref[...], kc_ref[...], kn_ref[...]], axis=0).astype(BF16)
    v = jnp.concatenate([vp_ref[...], vc_ref[...], vn_ref[...]], axis=0).astype(BF16)
    row = lax.broadcasted_iota(jnp.int32, (BLOCK, 3 * BLOCK), 0)
    col = lax.broadcasted_iota(jnp.int32, (BLOCK, 3 * BLOCK), 1)
    dist = jnp.abs(col - BLOCK - row)
    lo = jnp.where(n > 0, 0, BLOCK)
    hi = jnp.where(n < nb - 1, 3 * BLOCK, 2 * BLOCK)
    valid = (dist <= WINDOW) & (col >= lo) & (col < hi)
    distf = dist.astype(F32)
    scale = HEAD_DIM ** -0.5
    for r in range(KV_RATIO):
        h = g * KV_RATIO + r
        q = q_ref[:, r * HEAD_DIM:(r + 1) * HEAD_DIM].astype(BF16)
        s = lax.dot_general(q, k, (((1,), (1,)), ((), ())), preferred_element_type=F32) * scale
        s = jnp.where(valid, s - slope_ref[h] * distf, NEG_INF)
        sink = sink_ref[h]
        mx = jnp.maximum(jnp.max(s, axis=-1, keepdims=True), sink)
        p = jnp.exp(s - mx)
        denom = jnp.sum(p, axis=-1, keepdims=True) + jnp.exp(sink - mx)
        o = jnp.dot(p.astype(BF16), v, preferred_element_type=F32) / denom
        o_ref[:, r * HEAD_DIM:(r + 1) * HEAD_DIM] = o.astype(o_ref.dtype)


def banded_attention(z, sink, slopes, n_heads):
    b, l, _ = z.shape
    g = n_heads // KV_RATIO
    nb = l // BLOCK
    qw = KV_RATIO * HEAD_DIM
    kcol = n_heads
    vcol = n_heads + g

    def kv_spec(col0, shift):
        def imap(bi, ni, gi):
            return (bi, jnp.clip(ni + shift, 0, nb - 1), col0 + gi)
        return pl.BlockSpec((None, BLOCK, HEAD_DIM), imap)

    smem = pl.BlockSpec(memory_space=pltpu.SMEM)
    return pl.pallas_call(
        _attn_kernel,
        out_shape=jax.ShapeDtypeStruct((b, l, n_heads * HEAD_DIM), BF16),
        grid=(b, nb, g),
        in_specs=[smem, smem,
                  pl.BlockSpec((None, BLOCK, qw), lambda bi, ni, gi: (bi, ni, gi)),
                  kv_spec(kcol, -1), kv_spec(kcol, 0), kv_spec(kcol, 1),
                  kv_spec(vcol, -1), kv_spec(vcol, 0), kv_spec(vcol, 1)],
        out_specs=pl.BlockSpec((None, BLOCK, qw), lambda bi, ni, gi: (bi, ni, gi)),
        compiler_params=_params("parallel", "parallel", "arbitrary"),
        name="banded_attention",
    )(sink.astype(F32), slopes, z, z, z, z, z, z, z)


def _split_len(n):
    lg = int(math.log2(n))
    assert 2 ** lg == n
    n2 = 2 ** ((lg + 1) // 2)
    return n // n2, n2


def _cos_sin(n):
    idx = np.arange(n)
    ang = 2.0 * np.pi * ((idx[:, None] * idx[None, :]) % n) / n
    return np.cos(ang), np.sin(ang)


def _twiddle(n1, n2, sign):
    n = n1 * n2
    ang = 2.0 * np.pi * ((np.arange(n1)[:, None] * np.arange(n2)[None, :]) % n) / n
    return (jnp.asarray(np.cos(ang)[:, :, None], F32), jnp.asarray(sign * np.sin(ang)[:, :, None], F32))


def _fft1_kernel(x_ref, s_ref, f_ref, twr_ref, twi_ref, ar_ref, ai_ref, *, n2):
    x = (x_ref[...] * s_ref[...]).astype(BF16)
    a = jnp.dot(f_ref[...], x, preferred_element_type=F32)
    ar, ai = a[:n2], a[n2:]
    c, s = twr_ref[...], twi_ref[...]
    ar_ref[...] = (ar * c - ai * s).astype(ar_ref.dtype)
    ai_ref[...] = (ar * s + ai * c).astype(ai_ref.dtype)


def fft_stage1(x, col0, width, n1, n2, rows, scale, out_dtype):
    b, _, w = x.shape
    tc = _pick(math.gcd(math.gcd(w, col0) if col0 else w, width), (512, 256, 128))
    nct = width // tc
    cr, sr = _cos_sin(n2)
    f = jnp.asarray(np.concatenate([cr, -sr], axis=0)[:, :rows], BF16)
    twr, twi = _twiddle(n1, n2, -1.0)
    xv = x.reshape(b, rows, n1 * w)
    out = jax.ShapeDtypeStruct((b, n2, n1 * width), out_dtype)
    ospec = pl.BlockSpec((None, n2, tc), lambda bi, ni, ci: (bi, 0, ni * nct + ci))
    return pl.pallas_call(
        functools.partial(_fft1_kernel, n2=n2),
        out_shape=(out, out),
        grid=(b, n1, nct),
        in_specs=[pl.BlockSpec((None, rows, tc), lambda bi, ni, ci: (bi, 0, ni * (w // tc) + col0 // tc + ci)),
                  pl.BlockSpec((1, tc), lambda bi, ni, ci: (0, ci)),
                  pl.BlockSpec((2 * n2, rows), lambda bi, ni, ci: (0, 0)),
                  pl.BlockSpec((None, n2, 1), lambda bi, ni, ci: (ni, 0, 0)),
                  pl.BlockSpec((None, n2, 1), lambda bi, ni, ci: (ni, 0, 0))],
        out_specs=(ospec, ospec),
        compiler_params=_params("parallel", "parallel", "arbitrary"),
        name="fft_stage1",
    )(xv, scale, f, twr, twi)


def _cplx_dft_matrix(n, sign, scale=1.0):
    c, s = _cos_sin(n)
    s = -sign * s
    return np.block([[c, s], [-s, c]]) * scale


def _fourier2_kernel(ar_ref, ai_ref, m_ref, cs_ref, o_ref, *, n1, groups):
    a = jnp.concatenate([ar_ref[...], ai_ref[...]], axis=0)
    gm = jnp.dot(m_ref[...], a, preferred_element_type=F32)
    gr, gi = gm[:n1].astype(BF16), gm[n1:].astype(BF16)
    for q in range(groups):
        sl = slice(q * HEAD_DIM, (q + 1) * HEAD_DIM)
        lhs = jnp.concatenate([gr[:, sl], gi[:, sl]], axis=1)
        o_ref[:, sl] = jnp.dot(lhs, cs_ref[...], preferred_element_type=F32).astype(o_ref.dtype)


def fourier_mix(z, col0, width):
    b, l, _ = z.shape
    n1, n2 = _split_len(l)
    ones = jnp.ones((1, width), F32)
    ar, ai = fft_stage1(z, col0, width, n1, n2, n2, ones, BF16)
    m = jnp.asarray(_cplx_dft_matrix(n1, -1.0), BF16)
    cc, sc = _cos_sin(HEAD_DIM)
    cs = jnp.asarray(np.concatenate([cc, sc], axis=0) / math.sqrt(l * HEAD_DIM), BF16)
    groups = width // HEAD_DIM
    av = (b, n2, n1, width)
    out = pl.pallas_call(
        functools.partial(_fourier2_kernel, n1=n1, groups=groups),
        out_shape=jax.ShapeDtypeStruct((b, n1, n2 * width), BF16),
        grid=(b, n2),
        in_specs=[pl.BlockSpec((None, None, n1, width), lambda bi, ki: (bi, ki, 0, 0)),
                  pl.BlockSpec((None, None, n1, width), lambda bi, ki: (bi, ki, 0, 0)),
                  pl.BlockSpec((2 * n1, 2 * n1), lambda bi, ki: (0, 0)),
                  pl.BlockSpec((2 * HEAD_DIM, HEAD_DIM), lambda bi, ki: (0, 0))],
        out_specs=pl.BlockSpec((None, n1, width), lambda bi, ki: (bi, 0, ki)),
        compiler_params=_params("parallel", "arbitrary"),
        name="fourier_stage2",
    )(ar.reshape(av), ai.reshape(av), m, cs)
    return out.reshape(b, l, width)


def _conformer_kernel(ap_ref, ac_ref, an_ref, gp_ref, gc_ref, gn_ref, w_ref, b_ref, lg_ref, lb_ref,
                      o_ref, buf_ref, conv_ref, *, tl, width):
    t = pl.program_id(1)
    nt = pl.num_programs(1)
    h = CONV_HALO
    prev = ap_ref[...] * jax.nn.sigmoid(gp_ref[...])
    nxt = an_ref[...] * jax.nn.sigmoid(gn_ref[...])
    buf_ref[0:h, :] = jnp.where(t > 0, prev, 0.0)
    buf_ref[h:h + tl, :] = ac_ref[...] * jax.nn.sigmoid(gc_ref[...])
    buf_ref[h + tl:h + tl + h, :] = jnp.where(t < nt - 1, nxt, 0.0)
    base = h - CONV_KERNEL // 2
    for c in range(width // LANES):
        sl = slice(c * LANES, (c + 1) * LANES)
        acc = jnp.broadcast_to(b_ref[:, sl], (tl, LANES))
        for j in range(CONV_KERNEL):
            acc = acc + w_ref[j:j + 1, sl] * buf_ref[base + j:base + j + tl, sl]
        conv_ref[:, sl] = acc
    y = conv_ref[...]
    mu = jnp.mean(y, axis=-1, keepdims=True)
    yc = y - mu
    var = jnp.mean(yc * yc, axis=-1, keepdims=True)
    yn = yc * lax.rsqrt(var + EPS) * lg_ref[...] + lb_ref[...]
    o_ref[...] = (yn * jax.nn.sigmoid(yn)).astype(o_ref.dtype)


def conformer_conv(z, width, dw_w, dw_b, ln_g, ln_b):
    b, l, _ = z.shape
    tl = _pick(l, (128, 64, 32, 16))
    h = CONV_HALO
    r = tl // h
    nh = l // h

    def cur(col):
        return pl.BlockSpec((None, tl, width), lambda bi, ti: (bi, ti, col))

    def halo(col, nxt):
        if nxt:
            return pl.BlockSpec((None, h, width), lambda bi, ti: (bi, jnp.minimum((ti + 1) * r, nh - 1), col))
        return pl.BlockSpec((None, h, width), lambda bi, ti: (bi, jnp.maximum(ti * r - 1, 0), col))

    vec = pl.BlockSpec((1, width), lambda bi, ti: (0, 0))
    return pl.pallas_call(
        functools.partial(_conformer_kernel, tl=tl, width=width),
        out_shape=jax.ShapeDtypeStruct((b, l, width), BF16),
        grid=(b, l // tl),
        in_specs=[halo(0, False), cur(0), halo(0, True), halo(1, False), cur(1), halo(1, True),
                  pl.BlockSpec((CONV_KERNEL, width), lambda bi, ti: (0, 0)), vec, vec, vec],
        out_specs=pl.BlockSpec((None, tl, width), lambda bi, ti: (bi, ti, 0)),
        scratch_shapes=[pltpu.VMEM((tl + 2 * h, width), F32), pltpu.VMEM((tl, width), F32)],
        compiler_params=_params("parallel", "arbitrary"),
        name="conformer_conv",
    )(z, z, z, z, z, z, dw_w, dw_b.reshape(1, width), ln_g.reshape(1, width), ln_b.reshape(1, width))


def _short_conv(prev_ref, cur_ref, next_ref, w_ref, b_ref, first, last, tl):
    x = cur_ref[...]
    rows = lax.broadcasted_iota(jnp.int32, x.shape, 0)
    pr = jnp.where(first, 0.0, prev_ref[SUBLANES - 1:SUBLANES, :])
    nx = jnp.where(last, 0.0, next_ref[0:1, :])
    xm = jnp.where(rows == 0, pr, pltpu.roll(x, 1, axis=0))
    xp = jnp.where(rows == tl - 1, nx, pltpu.roll(x, tl - 1, axis=0))
    return w_ref[0:1, :] * xm + w_ref[1:2, :] * x + w_ref[2:3, :] * xp + b_ref[...]


def _hyena_gate_kernel(*refs, tl):
    x0 = refs[0:3]
    x1 = refs[3:6]
    xv = refs[6:9]
    w0, w1, wv, b0, b1, bv, x0_out, v_out = refs[9:]
    t = pl.program_id(1)
    first = t == 0
    last = t == pl.num_programs(1) - 1
    x0_out[...] = _short_conv(*x0, w0, b0, first, last, tl)
    v_out[...] = _short_conv(*xv, wv, bv, first, last, tl) * _short_conv(*x1, w1, b1, first, last, tl)


def hyena_gate(z, col0, width, sw, sb):
    b, l, _ = z.shape
    tl = _pick(l, (256, 128, 64, 32, 16, 8))
    tc = _pick(math.gcd(col0, width), (512, 256, 128))
    r = tl // SUBLANES
    nh = l // SUBLANES
    nct = width // tc

    def trio(part):
        c0 = (col0 + part * width) // tc
        return [pl.BlockSpec((None, SUBLANES, tc), lambda bi, ti, ci: (bi, jnp.maximum(ti * r - 1, 0), c0 + ci)),
                pl.BlockSpec((None, tl, tc), lambda bi, ti, ci: (bi, ti, c0 + ci)),
                pl.BlockSpec((None, SUBLANES, tc), lambda bi, ti, ci: (bi, jnp.minimum((ti + 1) * r, nh - 1), c0 + ci))]

    def wspec(part, rows):
        return pl.BlockSpec((rows, tc), lambda bi, ti, ci: (0, part * nct + ci))

    out = jax.ShapeDtypeStruct((b, l, width), F32)
    ospec = pl.BlockSpec((None, tl, tc), lambda bi, ti, ci: (bi, ti, ci))
    sb2 = sb.reshape(1, 3 * width)
    return pl.pallas_call(
        functools.partial(_hyena_gate_kernel, tl=tl),
        out_shape=(out, out),
        grid=(b, l // tl, nct),
        in_specs=trio(0) + trio(1) + trio(2) + [wspec(0, HYENA_SHORT), wspec(1, HYENA_SHORT), wspec(2, HYENA_SHORT),
                                                 wspec(0, 1), wspec(1, 1), wspec(2, 1)],
        out_specs=(ospec, ospec),
        compiler_params=_params("parallel", "parallel", "arbitrary"),
        name="hyena_gate",
    )(*([z] * 9), sw, sw, sw, sb2, sb2, sb2)


def _hdot(a, b):
    return jnp.dot(a, b, preferred_element_type=F32, precision=lax.Precision.HIGHEST)


def _filter_kernel(bands_ref, w1t_ref, w1c_ref, w1s_ref, b1_ref, w2_ref, b2_ref, w3_ref, b3_ref, fr_ref,
                   w4_ref, dec_ref, k_ref, norm_ref, *, l, tl):
    i = pl.program_id(0)
    m = i * tl + lax.broadcasted_iota(jnp.int32, (tl, 1), 0)
    j = jnp.where(m < l, m, 2 * l - m).astype(F32)
    t = j / (l - 1.0)
    ang = (2.0 * math.pi / l) * bands_ref[...] * j
    fr = fr_ref[...]
    pre = t * w1t_ref[...] + _hdot(jnp.cos(ang), w1c_ref[...]) - _hdot(jnp.sin(ang), w1s_ref[...])
    hcur = jnp.sin(fr * (pre + b1_ref[...]))
    hcur = jnp.sin(fr * (_hdot(hcur, w2_ref[...]) + b2_ref[...]))
    hcur = jnp.sin(fr * (_hdot(hcur, w3_ref[...]) + b3_ref[...]))
    out = _hdot(hcur, w4_ref[...]) * jnp.exp(-t * dec_ref[...])
    out = jnp.where(m == l, 0.0, out)
    k_ref[...] = out

    @pl.when(i == 0)
    def _():
        norm_ref[...] = jnp.zeros_like(norm_ref)

    norm_ref[...] += jnp.sum(jnp.abs(out), axis=0, keepdims=True)


def hyena_filter(l, width, w1, b1, w2, b2, w3, b3, freq, w4):
    hid = w1.shape[1]
    tl = _pick(l, (512, 256, 128, 64, 32, 16, 8))
    nl = l // tl
    bands = jnp.asarray(np.linspace(1e-4, HYENA_BANDS - 1, HYENA_BANDS, dtype=np.float32)[None, :])
    dec = jnp.asarray(np.abs(np.linspace(HYENA_MIN_DECAY, HYENA_MAX_DECAY, width, dtype=np.float32))[None, :])
    full = lambda shape: pl.BlockSpec(shape, lambda i: tuple(0 for _ in shape))
    w4v = w4.reshape(hid, 2, width).transpose(1, 0, 2)
    return pl.pallas_call(
        functools.partial(_filter_kernel, l=l, tl=tl),
        out_shape=(jax.ShapeDtypeStruct((2 * l, width), F32), jax.ShapeDtypeStruct((1, width), F32)),
        grid=(2 * nl,),
        in_specs=[full((1, HYENA_BANDS)), full((1, hid)), full((HYENA_BANDS, hid)), full((HYENA_BANDS, hid)),
                  full((1, hid)), full((hid, hid)), full((1, hid)), full((hid, hid)), full((1, hid)), full((1, hid)),
                  pl.BlockSpec((None, hid, width), lambda i: (i // nl, 0, 0)),
                  full((1, width))],
        out_specs=(pl.BlockSpec((tl, width), lambda i: (i, 0)), full((1, width))),
        compiler_params=_params("arbitrary"),
        name="hyena_filter",
    )(bands, w1[0:1], w1[1:1 + HYENA_BANDS], w1[1 + HYENA_BANDS:], b1.reshape(1, hid), w2, b2.reshape(1, hid),
      w3, b3.reshape(1, hid), freq.reshape(1, hid), w4v, dec)


def _spectrum2_kernel(ar_ref, ai_ref, m_ref, kr_ref, ki_ref, *, n1):
    a = jnp.concatenate([ar_ref[...], ai_ref[...]], axis=0)
    k = jnp.dot(m_ref[...], a, preferred_element_type=F32)
    kr_ref[...] = k[:n1]
    ki_ref[...] = k[n1:]


def filter_spectrum(kern, norm, n1, n2):
    nlen, width = kern.shape
    ar, ai = fft_stage1(kern.reshape(1, nlen, width), 0, width, n1, n2, n2, 1.0 / norm, BF16)
    m = jnp.asarray(_cplx_dft_matrix(n1, -1.0), BF16)
    av = (n2, n1, width)
    spec = pl.BlockSpec((None, n1, width), lambda ki: (ki, 0, 0))
    out = jax.ShapeDtypeStruct(av, F32)
    return pl.pallas_call(
        functools.partial(_spectrum2_kernel, n1=n1),
        out_shape=(out, out),
        grid=(n2,),
        in_specs=[spec, spec, pl.BlockSpec((2 * n1, 2 * n1), lambda ki: (0, 0))],
        out_specs=(spec, spec),
        compiler_params=_params("arbitrary"),
        name="filter_spectrum",
    )(ar.reshape(av), ai.reshape(av), m)


def _hyena_mid_kernel(ar_ref, ai_ref, kr_ref, ki_ref, mf_ref, mi_ref, twr_ref, twi_ref, br_ref, bi_ref, *, n1):
    a = jnp.concatenate([ar_ref[...], ai_ref[...]], axis=0)
    v = jnp.dot(mf_ref[...], a, preferred_element_type=F32)
    vr, vi = v[:n1], v[n1:]
    kr, ki = kr_ref[...], ki_ref[...]
    y = jnp.concatenate([vr * kr - vi * ki, vr * ki + vi * kr], axis=0).astype(BF16)
    bm = jnp.dot(mi_ref[...], y, preferred_element_type=F32)
    br, bi = bm[:n1], bm[n1:]
    c, s = twr_ref[...], twi_ref[...]
    br_ref[...] = (br * c - bi * s).astype(br_ref.dtype)
    bi_ref[...] = (br * s + bi * c).astype(bi_ref.dtype)


def _hyena_out_kernel(br_ref, bi_ref, f_ref, v_ref, x0_ref, bias_ref, o_ref):
    bm = jnp.concatenate([br_ref[...], bi_ref[...]], axis=0)
    y = jnp.dot(f_ref[...], bm, preferred_element_type=F32)
    o_ref[...] = (x0_ref[...] * (y + v_ref[...] * bias_ref[...])).astype(o_ref.dtype)


def hyena_long_conv(v, x0, kr, ki, bias, n1, n2):
    b, l, width = v.shape
    nlen = n1 * n2
    half = n2 // 2
    ones = jnp.ones((1, width), F32)
    ar, ai = fft_stage1(v, 0, width, n1, n2, half, ones, BF16)
    av = (b, n2, n1, width)
    mf = jnp.asarray(_cplx_dft_matrix(n1, -1.0), BF16)
    mi = jnp.asarray(_cplx_dft_matrix(n1, 1.0), BF16)
    twr, twi = _twiddle(n2, n1, 1.0)
    aspec = pl.BlockSpec((None, None, n1, width), lambda bi, ki_: (bi, ki_, 0, 0))
    kspec = pl.BlockSpec((None, n1, width), lambda bi, ki_: (ki_, 0, 0))
    mspec = pl.BlockSpec((2 * n1, 2 * n1), lambda bi, ki_: (0, 0))
    tspec = pl.BlockSpec((None, n1, 1), lambda bi, ki_: (ki_, 0, 0))
    bout = jax.ShapeDtypeStruct(av, BF16)
    br, bi_ = pl.pallas_call(
        functools.partial(_hyena_mid_kernel, n1=n1),
        out_shape=(bout, bout),
        grid=(b, n2),
        in_specs=[aspec, aspec, kspec, kspec, mspec, mspec, tspec, tspec],
        out_specs=(aspec, aspec),
        compiler_params=_params("parallel", "arbitrary"),
        name="hyena_mid",
    )(ar.reshape(av), ai.reshape(av), kr, ki, mf, mi, twr, twi)

    tc = _pick(width, (512, 256, 128))
    nct = width // tc
    cr, sr = _cos_sin(n2)
    finv = jnp.asarray(np.concatenate([cr, -sr], axis=1)[:half] / nlen, BF16)
    bv = (b, n2, n1 * width)
    sv = (b, half, n1 * width)
    bspec = pl.BlockSpec((None, n2, tc), lambda bi, ti, ci: (bi, 0, ti * nct + ci))
    sspec = pl.BlockSpec((None, half, tc), lambda bi, ti, ci: (bi, 0, ti * nct + ci))
    out = pl.pallas_call(
        _hyena_out_kernel,
        out_shape=jax.ShapeDtypeStruct(sv, BF16),
        grid=(b, n1, nct),
        in_specs=[bspec, bspec, pl.BlockSpec((half, 2 * n2), lambda bi, ti, ci: (0, 0)),
                  sspec, sspec, pl.BlockSpec((1, tc), lambda bi, ti, ci: (0, ci))],
        out_specs=sspec,
        compiler_params=_params("parallel", "parallel", "arbitrary"),
        name="hyena_out",
    )(br.reshape(bv), bi_.reshape(bv), finv, v.reshape(sv), x0.reshape(sv), bias.reshape(1, width))
    return out.reshape(b, l, width)


def _trunk(x, p, wb):
    b, l, d = x.shape
    m = b * l
    depth = p['mix_norm'].shape[0]
    fourier_w = d // 4
    attn_w = d - fourier_w
    n_heads = attn_w // HEAD_DIM
    kv_w = (n_heads // KV_RATIO) * HEAD_DIM
    conv_w = d // 2
    hy_w = d - conv_w
    slopes = jnp.asarray(alibi_slopes(n_heads))
    hn1, hn2 = _split_len(2 * l)

    xf = x.reshape(m, d)
    for layer in range(depth):
        i = layer // 2
        h = rms_norm(xf, p['mix_norm'][layer], BF16)
        if layer % 2 == 0:
            z = proj(h, wb['ab_w_in'], i, F32).reshape(b, l, -1)
            a = banded_attention(z, p['attn_sink'][i], slopes, n_heads)
            f = fourier_mix(z, attn_w + 2 * kv_w, fourier_w)
            parts = [a.reshape(m, attn_w), f.reshape(m, fourier_w)]
            xf = out_proj(parts, wb['ab_w_out'], i, xf)
        else:
            z = proj(h, wb['cd_w_in'], i, F32).reshape(b, l, -1)
            c = conformer_conv(z, conv_w, p['conv_dw_w'][i], p['conv_dw_b'][i], p['conv_ln_g'][i], p['conv_ln_b'][i])
            x0, v = hyena_gate(z, 2 * conv_w, hy_w, p['hy_short_w'][i], p['hy_short_b'][i])
            kern, norm = hyena_filter(l, hy_w, p['hy_filt_w1'][i], p['hy_filt_b1'][i], p['hy_filt_w2'][i],
                                      p['hy_filt_b2'][i], p['hy_filt_w3'][i], p['hy_filt_b3'][i],
                                      p['hy_filt_freq'][i], p['hy_filt_w4'][i])
            kr, ki = filter_spectrum(kern, norm, hn1, hn2)
            dd = hyena_long_conv(v, x0, kr, ki, p['hy_bias'][i], hn1, hn2)
            parts = [c.reshape(m, conv_w), dd.reshape(m, hy_w)]
            xf = out_proj(parts, wb['cd_w_out'], i, xf)
        h = rms_norm(xf, p['ffn_norm'][layer], BF16)
        hid = swiglu_in(h, wb['w_gate'], wb['w_up'], layer)
        xf = out_proj([hid], wb['w_down'], layer, xf)
    return rms_norm(xf, p['final_norm'], F32).reshape(b, l, d)


def kernel(x_prompt, x_sample, mix_norm, ffn_norm, final_norm, w_gate, w_up, w_down, ab_w_in, ab_w_out, attn_sink, cd_w_in, cd_w_out, conv_dw_w, conv_dw_b, conv_ln_g, conv_ln_b, hy_short_w, hy_short_b, hy_filt_w1, hy_filt_b1, hy_filt_w2, hy_filt_b2, hy_filt_w3, hy_filt_b3, hy_filt_freq, hy_filt_w4, hy_bias):
    p = dict(mix_norm=mix_norm, ffn_norm=ffn_norm, final_norm=final_norm, attn_sink=attn_sink,
             conv_dw_w=conv_dw_w, conv_dw_b=conv_dw_b, conv_ln_g=conv_ln_g, conv_ln_b=conv_ln_b,
             hy_short_w=hy_short_w, hy_short_b=hy_short_b, hy_filt_w1=hy_filt_w1, hy_filt_b1=hy_filt_b1,
             hy_filt_w2=hy_filt_w2, hy_filt_b2=hy_filt_b2, hy_filt_w3=hy_filt_w3, hy_filt_b3=hy_filt_b3,
             hy_filt_freq=hy_filt_freq, hy_filt_w4=hy_filt_w4, hy_bias=hy_bias)
    wb = dict(w_gate=w_gate.astype(BF16), w_up=w_up.astype(BF16), w_down=w_down.astype(BF16),
              ab_w_in=ab_w_in.astype(BF16), ab_w_out=ab_w_out.astype(BF16),
              cd_w_in=cd_w_in.astype(BF16), cd_w_out=cd_w_out.astype(BF16))
    return (_trunk(x_prompt, p, wb), _trunk(x_sample, p, wb))
```

```python
import functools
import math

import numpy as np
import jax
import jax.numpy as jnp
from jax import lax
from jax.experimental import pallas as pl
from jax.experimental.pallas import tpu as pltpu

F32 = jnp.float32
BF16 = jnp.bfloat16
U32 = jnp.uint32

HEAD_DIM = 128
KV_RATIO = 4
WINDOW = 128
BLOCK = 128
CONV_KERNEL = 31
CONV_HALO = 16
HYENA_SHORT = 3
HYENA_BANDS = 16
HYENA_MIN_DECAY = math.log(1e-2) / 1.5
HYENA_MAX_DECAY = math.log(1e-2) / 0.3
EPS = 1e-6
NEG_INF = -1e30

V7X_VMEM_BYTES = 64 * 1024 * 1024
VMEM_LIMIT = V7X_VMEM_BYTES - 8 * 1024 * 1024
LANES = 128
SUBLANES = 8


def _params(*sem):
    return pltpu.CompilerParams(dimension_semantics=sem, vmem_limit_bytes=VMEM_LIMIT)


def _pick(n, candidates):
    for c in candidates:
        if c <= n and n % c == 0:
            return c
    return n


def _rms_kernel(x_ref, g_ref, o_ref):
    x = x_ref[...]
    ms = jnp.mean(x * x, axis=-1, keepdims=True)
    o_ref[...] = (x * lax.rsqrt(ms + EPS) * g_ref[...]).astype(o_ref.dtype)


def rms_norm(x, g, out_dtype):
    m, d = x.shape
    tr = _pick(m, (256, 128, 64, 32, 16, 8))
    return pl.pallas_call(
        _rms_kernel,
        out_shape=jax.ShapeDtypeStruct((m, d), out_dtype),
        grid=(m // tr,),
        in_specs=[pl.BlockSpec((tr, d), lambda i: (i, 0)),
                  pl.BlockSpec((1, d), lambda i: (0, 0))],
        out_specs=pl.BlockSpec((tr, d), lambda i: (i, 0)),
        compiler_params=_params("parallel"),
        name="rms_norm",
    )(x, g.reshape(1, d))


def _proj_kernel(a_ref, w_ref, o_ref):
    o_ref[...] = jnp.dot(a_ref[...], w_ref[...], preferred_element_type=F32).astype(o_ref.dtype)


def proj(a, w, layer, out_dtype):
    m, k = a.shape
    n = w.shape[-1]
    tm = _pick(m, (1024, 512, 256, 128))
    tn = _pick(n, (1024, 512, 256, 128))
    return pl.pallas_call(
        _proj_kernel,
        out_shape=jax.ShapeDtypeStruct((m, n), out_dtype),
        grid=(m // tm, n // tn),
        in_specs=[pl.BlockSpec((tm, k), lambda i, j: (i, 0)),
                  pl.BlockSpec((None, k, tn), lambda i, j: (layer, 0, j))],
        out_specs=pl.BlockSpec((tm, tn), lambda i, j: (i, j)),
        compiler_params=_params("parallel", "arbitrary"),
        name="proj",
    )(a, w)


def _swiglu_kernel(a_ref, wg_ref, wu_ref, o_ref):
    a = a_ref[...]
    g = jnp.dot(a, wg_ref[...], preferred_element_type=F32)
    u = jnp.dot(a, wu_ref[...], preferred_element_type=F32)
    o_ref[...] = (g * jax.nn.sigmoid(g) * u).astype(o_ref.dtype)


def swiglu_in(a, wg, wu, layer):
    m, k = a.shape
    n = wg.shape[-1]
    tm = _pick(m, (1024, 512, 256, 128))
    tn = _pick(n, (512, 256, 128))
    return pl.pallas_call(
        _swiglu_kernel,
        out_shape=jax.ShapeDtypeStruct((m, n), BF16),
        grid=(m // tm, n // tn),
        in_specs=[pl.BlockSpec((tm, k), lambda i, j: (i, 0)),
                  pl.BlockSpec((None, k, tn), lambda i, j: (layer, 0, j)),
                  pl.BlockSpec((None, k, tn), lambda i, j: (layer, 0, j))],
        out_specs=pl.BlockSpec((tm, tn), lambda i, j: (i, j)),
        compiler_params=_params("parallel", "arbitrary"),
        name="swiglu_in",
    )(a, wg, wu)


def _out_kernel(*refs, dtypes):
    n_parts = len(dtypes)
    a_refs = refs[:n_parts]
    w_refs = refs[n_parts:2 * n_parts]
    res_ref = refs[2 * n_parts]
    o_ref = refs[2 * n_parts + 1]
    cast_refs = list(refs[2 * n_parts + 2:])

    @pl.when(pl.program_id(1) == 0)
    def _():
        k = 0
        for a_ref, dt in zip(a_refs, dtypes):
            if dt != BF16:
                cast_refs[k][...] = a_ref[...].astype(BF16)
                k += 1

    acc = res_ref[...]
    k = 0
    for a_ref, w_ref, dt in zip(a_refs, w_refs, dtypes):
        if dt != BF16:
            a = cast_refs[k][...]
            k += 1
        else:
            a = a_ref[...]
        acc = acc + jnp.dot(a, w_ref[...], preferred_element_type=F32)
    o_ref[...] = acc


def out_proj(parts, w, layer, res):
    m = res.shape[0]
    n = w.shape[-1]
    kp = [p.shape[1] for p in parts]
    ksum = sum(kp)
    tm = _pick(m, (1024, 512, 256, 128))
    tn = _pick(n, (512, 256, 128)) if ksum <= 4096 else _pick(n, (256, 128))
    single = ksum * tm * 2 > 12 * 1024 * 1024
    in_specs = []
    for k in kp:
        if single:
            in_specs.append(pl.BlockSpec((tm, k), lambda i, j: (i, 0), pipeline_mode=pl.Buffered(1)))
        else:
            in_specs.append(pl.BlockSpec((tm, k), lambda i, j: (i, 0)))
    off = 0
    for k in kp:
        assert off % k == 0, "each part must start at a multiple of its own width"
        blk = off // k
        in_specs.append(pl.BlockSpec((None, k, tn), lambda i, j, blk=blk: (layer, blk, j)))
        off += k
    in_specs.append(pl.BlockSpec((tm, tn), lambda i, j: (i, j)))
    dtypes = tuple(p.dtype for p in parts)
    scratch = [pltpu.VMEM((tm, k), BF16) for k, dt in zip(kp, dtypes) if dt != BF16]
    return pl.pallas_call(
        functools.partial(_out_kernel, dtypes=dtypes),
        out_shape=jax.ShapeDtypeStruct((m, n), F32),
        grid=(m // tm, n // tn),
        in_specs=in_specs,
        out_specs=pl.BlockSpec((tm, tn), lambda i, j: (i, j)),
        scratch_shapes=scratch,
        compiler_params=_params("parallel", "arbitrary"),
        name="out_proj",
    )(*parts, *([w] * len(parts)), res)


def alibi_slopes(n):
    def pow2(m):
        start = 2.0 ** (-(2.0 ** -(math.log2(m) - 3)))
        return [start ** (i + 1) for i in range(m)]
    if math.log2(n).is_integer():
        s = pow2(n)
    else:
        c = 2 ** math.floor(math.log2(n))
        s = pow2(c) + pow2(2 * c)[0::2][: n - c]
    return np.asarray(s, np.float32)


def _attn_kernel(sink_ref, slope_ref, q_ref, kp_ref, kc_ref, kn_ref, vp_ref, vc_ref, vn_ref, o_ref, *, nsub):
    n = pl.program_id(1)
    g = pl.program_id(2)
    nb = pl.num_programs(1) * nsub
    k = jnp.concatenate([kp_ref[...], kc_ref[...], kn_ref[...]], axis=0).astype(BF16)
    v = jnp.concatenate([vp_ref[...], vc_ref[...], vn_ref[...]], axis=0).astype(BF16)
    row = lax.broadcasted_iota(jnp.int32, (BLOCK, 3 * BLOCK), 0)
    col = lax.broadcasted_iota(jnp.int32, (BLOCK, 3 * BLOCK), 1)
    dist = jnp.abs(col - BLOCK - row)
    band = dist <= WINDOW
    distf = dist.astype(F32)
    scale = HEAD_DIM ** -0.5
    for r in range(KV_RATIO):
        h = g * KV_RATIO + r
        sink = sink_ref[h]
        alibi = slope_ref[h] * distf
        for sub in range(nsub):
            blk = n * nsub + sub
            rows = slice(sub * BLOCK, (sub + 1) * BLOCK)
            keys = slice(sub * BLOCK, (sub + 3) * BLOCK)
            q = q_ref[rows, r * HEAD_DIM:(r + 1) * HEAD_DIM].astype(BF16)
            s = lax.dot_general(q, k[keys], (((1,), (1,)), ((), ())), preferred_element_type=F32) * scale
            valid = band
            if sub == 0:
                valid = valid & (col >= jnp.where(blk > 0, 0, BLOCK))
            if sub == nsub - 1:
                valid = valid & (col < jnp.where(blk < nb - 1, 3 * BLOCK, 2 * BLOCK))
            s = jnp.where(valid, s - alibi, NEG_INF)
            mx = jnp.maximum(jnp.max(s, axis=-1, keepdims=True), sink)
            p = jnp.exp(s - mx)
            denom = jnp.sum(p, axis=-1, keepdims=True) + jnp.exp(sink - mx)
            o = jnp.dot(p.astype(BF16), v[keys], preferred_element_type=F32) / denom
            o_ref[rows, r * HEAD_DIM:(r + 1) * HEAD_DIM] = o.astype(o_ref.dtype)


def banded_attention(z, sink, slopes, n_heads):
    b, l, _ = z.shape
    g = n_heads // KV_RATIO
    tq = _pick(l, (512, 256, 128))
    nsub = tq // BLOCK
    nq = l // tq
    nb = l // BLOCK
    qw = KV_RATIO * HEAD_DIM
    kcol = n_heads
    vcol = n_heads + g

    def cur_spec(col0):
        return pl.BlockSpec((None, tq, HEAD_DIM), lambda bi, ni, gi: (bi, ni, col0 + gi))

    def halo_spec(col0, nxt):
        if nxt:
            return pl.BlockSpec((None, BLOCK, HEAD_DIM),
                                lambda bi, ni, gi: (bi, jnp.minimum((ni + 1) * nsub, nb - 1), col0 + gi))
        return pl.BlockSpec((None, BLOCK, HEAD_DIM),
                            lambda bi, ni, gi: (bi, jnp.maximum(ni * nsub - 1, 0), col0 + gi))

    smem = pl.BlockSpec(memory_space=pltpu.SMEM)
    return pl.pallas_call(
        functools.partial(_attn_kernel, nsub=nsub),
        out_shape=jax.ShapeDtypeStruct((b, l, n_heads * HEAD_DIM), BF16),
        grid=(b, nq, g),
        in_specs=[smem, smem,
                  pl.BlockSpec((None, tq, qw), lambda bi, ni, gi: (bi, ni, gi)),
                  halo_spec(kcol, False), cur_spec(kcol), halo_spec(kcol, True),
                  halo_spec(vcol, False), cur_spec(vcol), halo_spec(vcol, True)],
        out_specs=pl.BlockSpec((None, tq, qw), lambda bi, ni, gi: (bi, ni, gi)),
        compiler_params=_params("parallel", "parallel", "arbitrary"),
        name="banded_attention",
    )(sink.astype(F32), slopes, z, z, z, z, z, z, z)


def _split_len(n):
    lg = int(math.log2(n))
    assert 2 ** lg == n
    n2 = 2 ** ((lg + 1) // 2)
    return n // n2, n2


def _cos_sin(n):
    idx = np.arange(n)
    ang = 2.0 * np.pi * ((idx[:, None] * idx[None, :]) % n) / n
    return np.cos(ang), np.sin(ang)


def _twiddle(na, nb, sign):
    n = na * nb
    ang = 2.0 * np.pi * ((np.arange(na)[:, None] * np.arange(nb)[None, :]) % n) / n
    return (jnp.asarray(np.cos(ang)[:, :, None], F32), jnp.asarray(sign * np.sin(ang)[:, :, None], F32))


def _cplx_dft_matrix(n, sign, scale=1.0):
    c, s = _cos_sin(n)
    s = -sign * s
    return np.block([[c, s], [-s, c]]) * scale


def _pack_c(re, im):
    r = lax.bitcast_convert_type(re.astype(BF16).astype(F32), U32)
    i = lax.bitcast_convert_type(im.astype(BF16).astype(F32), U32)
    return r | (i >> 16)


def _unpack_c(p):
    re = lax.bitcast_convert_type(p & jnp.uint32(0xFFFF0000), F32)
    im = lax.bitcast_convert_type(p << 16, F32)
    return jnp.concatenate([re, im], axis=0).astype(BF16)


def _fft1_kernel(x_ref, s_ref, f_ref, twr_ref, twi_ref, a_ref, *, n2):
    scale = s_ref[...]
    f = f_ref[...]
    xt = pltpu.einshape("rsc->src", x_ref[...])
    outs = []
    for s in range(SUBLANES):
        x = (xt[s] * scale).astype(BF16)
        a = jnp.dot(f, x, preferred_element_type=F32)
        ar, ai = a[:n2], a[n2:]
        c, sn = twr_ref[s], twi_ref[s]
        outs.append(_pack_c(ar * c - ai * sn, ar * sn + ai * c))
    a_ref[...] = pltpu.einshape("src->rsc", jnp.stack(outs, axis=0))


def fft_stage1(x, col0, width, n1, n2, rows, scale):
    b, _, w = x.shape
    tc = _pick(math.gcd(math.gcd(w, col0) if col0 else w, width), (512, 256, 128))
    nct = width // tc
    c0 = col0 // tc
    cr, sr = _cos_sin(n2)
    f = jnp.asarray(np.concatenate([cr, -sr], axis=0)[:, :rows], BF16)
    twr, twi = _twiddle(n1, n2, -1.0)
    tspec = pl.BlockSpec((SUBLANES, n2, 1), lambda bi, ni, ci: (ni, 0, 0))
    return pl.pallas_call(
        functools.partial(_fft1_kernel, n2=n2),
        out_shape=jax.ShapeDtypeStruct((b, n2, n1, width), U32),
        grid=(b, n1 // SUBLANES, nct),
        in_specs=[pl.BlockSpec((None, rows, SUBLANES, tc), lambda bi, ni, ci: (bi, 0, ni, c0 + ci)),
                  pl.BlockSpec((1, tc), lambda bi, ni, ci: (0, ci)),
                  pl.BlockSpec((2 * n2, rows), lambda bi, ni, ci: (0, 0)),
                  tspec, tspec],
        out_specs=pl.BlockSpec((None, n2, SUBLANES, tc), lambda bi, ni, ci: (bi, 0, ni, ci)),
        compiler_params=_params("parallel", "parallel", "arbitrary"),
        name="fft_stage1",
    )(x.reshape(b, rows, n1, w), scale, f, twr, twi)


def _fourier2_kernel(a_ref, m_ref, cs_ref, o_ref, *, n1, groups):
    outs = []
    for s in range(SUBLANES):
        gm = jnp.dot(m_ref[...], _unpack_c(a_ref[s]), preferred_element_type=F32)
        gr, gi = gm[:n1].astype(BF16), gm[n1:].astype(BF16)
        cols = []
        for q in range(groups):
            sl = slice(q * HEAD_DIM, (q + 1) * HEAD_DIM)
            lhs = jnp.concatenate([gr[:, sl], gi[:, sl]], axis=1)
            cols.append(jnp.dot(lhs, cs_ref[...], preferred_element_type=F32))
        outs.append(jnp.concatenate(cols, axis=1))
    o_ref[...] = pltpu.einshape("src->rsc", jnp.stack(outs, axis=0))


def fourier_mix(z, col0, width):
    b, l, _ = z.shape
    n1, n2 = _split_len(l)
    ones = jnp.ones((1, width), F32)
    a = fft_stage1(z, col0, width, n1, n2, n2, ones)
    m = jnp.asarray(_cplx_dft_matrix(n1, -1.0), BF16)
    cc, sc = _cos_sin(HEAD_DIM)
    cs = jnp.asarray(np.concatenate([cc, sc], axis=0) / math.sqrt(l * HEAD_DIM), BF16)
    groups = width // HEAD_DIM
    out = pl.pallas_call(
        functools.partial(_fourier2_kernel, n1=n1, groups=groups),
        out_shape=jax.ShapeDtypeStruct((b, n1, n2, width), F32),
        grid=(b, n2 // SUBLANES),
        in_specs=[pl.BlockSpec((None, SUBLANES, n1, width), lambda bi, ki: (bi, ki, 0, 0)),
                  pl.BlockSpec((2 * n1, 2 * n1), lambda bi, ki: (0, 0)),
                  pl.BlockSpec((2 * HEAD_DIM, HEAD_DIM), lambda bi, ki: (0, 0))],
        out_specs=pl.BlockSpec((None, n1, SUBLANES, width), lambda bi, ki: (bi, 0, ki, 0)),
        compiler_params=_params("parallel", "arbitrary"),
        name="fourier_stage2",
    )(a, m, cs)
    return out.reshape(b, l, width)


def _conformer_kernel(ap_ref, ac_ref, an_ref, gp_ref, gc_ref, gn_ref, w_ref, b_ref, lg_ref, lb_ref,
                      o_ref, buf_ref, sh_ref, conv_ref, *, tl, width):
    t = pl.program_id(1)
    nt = pl.num_programs(1)
    h = CONV_HALO
    prev = ap_ref[...] * jax.nn.sigmoid(gp_ref[...])
    nxt = an_ref[...] * jax.nn.sigmoid(gn_ref[...])
    buf_ref[0:h, :] = jnp.where(t > 0, prev, 0.0)
    buf_ref[h:h + tl, :] = ac_ref[...] * jax.nn.sigmoid(gc_ref[...])
    buf_ref[h + tl:h + tl + h, :] = jnp.where(t < nt - 1, nxt, 0.0)
    span = tl + 2 * h - SUBLANES
    for ph in range(SUBLANES):
        sh_ref[ph, 0:span, :] = buf_ref[ph:ph + span, :]
    base = h - CONV_KERNEL // 2
    for c in range(width // LANES):
        sl = slice(c * LANES, (c + 1) * LANES)
        acc = jnp.broadcast_to(b_ref[:, sl], (tl, LANES))
        for j in range(CONV_KERNEL):
            a8, ph = divmod(base + j, SUBLANES)
            acc = acc + w_ref[j:j + 1, sl] * sh_ref[ph, a8 * SUBLANES:a8 * SUBLANES + tl, sl]
        conv_ref[:, sl] = acc
    y = conv_ref[...]
    mu = jnp.mean(y, axis=-1, keepdims=True)
    yc = y - mu
    var = jnp.mean(yc * yc, axis=-1, keepdims=True)
    yn = yc * lax.rsqrt(var + EPS) * lg_ref[...] + lb_ref[...]
    o_ref[...] = (yn * jax.nn.sigmoid(yn)).astype(o_ref.dtype)


def conformer_conv(z, width, dw_w, dw_b, ln_g, ln_b):
    b, l, _ = z.shape
    tl = _pick(l, (128, 64, 32, 16))
    h = CONV_HALO
    r = tl // h
    nh = l // h

    def cur(col):
        return pl.BlockSpec((None, tl, width), lambda bi, ti: (bi, ti, col))

    def halo(col, nxt):
        if nxt:
            return pl.BlockSpec((None, h, width), lambda bi, ti: (bi, jnp.minimum((ti + 1) * r, nh - 1), col))
        return pl.BlockSpec((None, h, width), lambda bi, ti: (bi, jnp.maximum(ti * r - 1, 0), col))

    vec = pl.BlockSpec((1, width), lambda bi, ti: (0, 0))
    return pl.pallas_call(
        functools.partial(_conformer_kernel, tl=tl, width=width),
        out_shape=jax.ShapeDtypeStruct((b, l, width), BF16),
        grid=(b, l // tl),
        in_specs=[halo(0, False), cur(0), halo(0, True), halo(1, False), cur(1), halo(1, True),
                  pl.BlockSpec((CONV_KERNEL, width), lambda bi, ti: (0, 0)), vec, vec, vec],
        out_specs=pl.BlockSpec((None, tl, width), lambda bi, ti: (bi, ti, 0)),
        scratch_shapes=[pltpu.VMEM((tl + 2 * h, width), F32),
                        pltpu.VMEM((SUBLANES, tl + 2 * h, width), F32),
                        pltpu.VMEM((tl, width), F32)],
        compiler_params=_params("parallel", "arbitrary"),
        name="conformer_conv",
    )(z, z, z, z, z, z, dw_w, dw_b.reshape(1, width), ln_g.reshape(1, width), ln_b.reshape(1, width))


def _short_conv(prev_ref, cur_ref, next_ref, w_ref, b_ref, first, last, tl):
    x = cur_ref[...]
    rows = lax.broadcasted_iota(jnp.int32, x.shape, 0)
    pr = jnp.where(first, 0.0, prev_ref[SUBLANES - 1:SUBLANES, :])
    nx = jnp.where(last, 0.0, next_ref[0:1, :])
    xm = jnp.where(rows == 0, pr, pltpu.roll(x, 1, axis=0))
    xp = jnp.where(rows == tl - 1, nx, pltpu.roll(x, tl - 1, axis=0))
    return w_ref[0:1, :] * xm + w_ref[1:2, :] * x + w_ref[2:3, :] * xp + b_ref[...]


def _hyena_gate_kernel(*refs, tl):
    x0 = refs[0:3]
    x1 = refs[3:6]
    xv = refs[6:9]
    w0, w1, wv, b0, b1, bv, x0_out, v_out = refs[9:]
    t = pl.program_id(1)
    first = t == 0
    last = t == pl.num_programs(1) - 1
    x0_out[...] = _short_conv(*x0, w0, b0, first, last, tl)
    v_out[...] = _short_conv(*xv, wv, bv, first, last, tl) * _short_conv(*x1, w1, b1, first, last, tl)


def hyena_gate(z, col0, width, sw, sb):
    b, l, _ = z.shape
    tl = _pick(l, (512, 256, 128, 64, 32, 16, 8))
    tc = _pick(math.gcd(col0, width), (512, 256, 128))
    r = tl // SUBLANES
    nh = l // SUBLANES
    nct = width // tc

    def trio(part):
        c0 = (col0 + part * width) // tc
        return [pl.BlockSpec((None, SUBLANES, tc), lambda bi, ti, ci: (bi, jnp.maximum(ti * r - 1, 0), c0 + ci)),
                pl.BlockSpec((None, tl, tc), lambda bi, ti, ci: (bi, ti, c0 + ci)),
                pl.BlockSpec((None, SUBLANES, tc), lambda bi, ti, ci: (bi, jnp.minimum((ti + 1) * r, nh - 1), c0 + ci))]

    def wspec(part, rows):
        return pl.BlockSpec((rows, tc), lambda bi, ti, ci: (0, part * nct + ci))

    out = jax.ShapeDtypeStruct((b, l, width), F32)
    ospec = pl.BlockSpec((None, tl, tc), lambda bi, ti, ci: (bi, ti, ci))
    sb2 = sb.reshape(1, 3 * width)
    return pl.pallas_call(
        functools.partial(_hyena_gate_kernel, tl=tl),
        out_shape=(out, out),
        grid=(b, l // tl, nct),
        in_specs=trio(0) + trio(1) + trio(2) + [wspec(0, HYENA_SHORT), wspec(1, HYENA_SHORT), wspec(2, HYENA_SHORT),
                                                 wspec(0, 1), wspec(1, 1), wspec(2, 1)],
        out_specs=(ospec, ospec),
        compiler_params=_params("parallel", "parallel", "arbitrary"),
        name="hyena_gate",
    )(*([z] * 9), sw, sw, sw, sb2, sb2, sb2)


def _hdot(a, b):
    return jnp.dot(a, b, preferred_element_type=F32, precision=lax.Precision.HIGHEST)


def _filter_kernel(bands_ref, w1t_ref, w1c_ref, w1s_ref, b1_ref, w2_ref, b2_ref, w3_ref, b3_ref, fr_ref,
                   w4_ref, dec_ref, k_ref, norm_ref, *, l, tl):
    i = pl.program_id(0)
    m = i * tl + lax.broadcasted_iota(jnp.int32, (tl, 1), 0)
    j = jnp.where(m < l, m, 2 * l - m).astype(F32)
    t = j / (l - 1.0)
    ang = (2.0 * math.pi / l) * bands_ref[...] * j
    fr = fr_ref[...]
    pre = t * w1t_ref[...] + _hdot(jnp.cos(ang), w1c_ref[...]) - _hdot(jnp.sin(ang), w1s_ref[...])
    hcur = jnp.sin(fr * (pre + b1_ref[...]))
    hcur = jnp.sin(fr * (_hdot(hcur, w2_ref[...]) + b2_ref[...]))
    hcur = jnp.sin(fr * (_hdot(hcur, w3_ref[...]) + b3_ref[...]))
    out = _hdot(hcur, w4_ref[...]) * jnp.exp(-t * dec_ref[...])
    out = jnp.where(m == l, 0.0, out)
    k_ref[...] = out

    @pl.when(i == 0)
    def _():
        norm_ref[...] = jnp.zeros_like(norm_ref)

    norm_ref[...] += jnp.sum(jnp.abs(out), axis=0, keepdims=True)


def hyena_filter(l, width, w1, b1, w2, b2, w3, b3, freq, w4):
    hid = w1.shape[1]
    tl = _pick(l, (512, 256, 128, 64, 32, 16, 8))
    nl = l // tl
    bands = jnp.asarray(np.linspace(1e-4, HYENA_BANDS - 1, HYENA_BANDS, dtype=np.float32)[None, :])
    dec = jnp.asarray(np.abs(np.linspace(HYENA_MIN_DECAY, HYENA_MAX_DECAY, width, dtype=np.float32))[None, :])
    full = lambda shape: pl.BlockSpec(shape, lambda i: tuple(0 for _ in shape))
    w4v = w4.reshape(hid, 2, width).transpose(1, 0, 2)
    return pl.pallas_call(
        functools.partial(_filter_kernel, l=l, tl=tl),
        out_shape=(jax.ShapeDtypeStruct((2 * l, width), F32), jax.ShapeDtypeStruct((1, width), F32)),
        grid=(2 * nl,),
        in_specs=[full((1, HYENA_BANDS)), full((1, hid)), full((HYENA_BANDS, hid)), full((HYENA_BANDS, hid)),
                  full((1, hid)), full((hid, hid)), full((1, hid)), full((hid, hid)), full((1, hid)), full((1, hid)),
                  pl.BlockSpec((None, hid, width), lambda i: (i // nl, 0, 0)),
                  full((1, width))],
        out_specs=(pl.BlockSpec((tl, width), lambda i: (i, 0)), full((1, width))),
        compiler_params=_params("arbitrary"),
        name="hyena_filter",
    )(bands, w1[0:1], w1[1:1 + HYENA_BANDS], w1[1 + HYENA_BANDS:], b1.reshape(1, hid), w2, b2.reshape(1, hid),
      w3, b3.reshape(1, hid), freq.reshape(1, hid), w4v, dec)


def _spectrum2_kernel(a_ref, m_ref, k_ref, *, n1, kb):
    for s in range(kb):
        k = jnp.dot(m_ref[...], _unpack_c(a_ref[s]), preferred_element_type=F32)
        k_ref[s] = _pack_c(k[:n1], k[n1:])


def _mid_block(n2):
    return 2 if n2 % 2 == 0 else 1


def filter_spectrum(kern, norm, n1, n2):
    nlen, width = kern.shape
    a = fft_stage1(kern.reshape(1, nlen, width), 0, width, n1, n2, n2, 1.0 / norm)
    m = jnp.asarray(_cplx_dft_matrix(n1, -1.0), BF16)
    kb = _mid_block(n2)
    spec = pl.BlockSpec((kb, n1, width), lambda ki: (ki, 0, 0))
    return pl.pallas_call(
        functools.partial(_spectrum2_kernel, n1=n1, kb=kb),
        out_shape=jax.ShapeDtypeStruct((n2, n1, width), U32),
        grid=(n2 // kb,),
        in_specs=[spec, pl.BlockSpec((2 * n1, 2 * n1), lambda ki: (0, 0))],
        out_specs=spec,
        compiler_params=_params("arbitrary"),
        name="filter_spectrum",
    )(a.reshape(n2, n1, width), m)


def _hyena_mid_kernel(a_ref, k_ref, mf_ref, mi_ref, twr_ref, twi_ref, b_ref, *, n1, kb):
    for s in range(kb):
        v = jnp.dot(mf_ref[...], _unpack_c(a_ref[s]), preferred_element_type=F32)
        vr, vi = v[:n1], v[n1:]
        kp = k_ref[s]
        kr = lax.bitcast_convert_type(kp & jnp.uint32(0xFFFF0000), F32)
        ki = lax.bitcast_convert_type(kp << 16, F32)
        y = jnp.concatenate([vr * kr - vi * ki, vr * ki + vi * kr], axis=0).astype(BF16)
        bm = jnp.dot(mi_ref[...], y, preferred_element_type=F32)
        br, bi = bm[:n1], bm[n1:]
        c, sn = twr_ref[s], twi_ref[s]
        b_ref[s] = _pack_c(br * c - bi * sn, br * sn + bi * c)


def _hyena_out_kernel(b_ref, f_ref, v_ref, x0_ref, bias_ref, o_ref):
    bias = bias_ref[...]
    f = f_ref[...]
    bt = pltpu.einshape("rsc->src", b_ref[...])
    ys = [jnp.dot(f, _unpack_c(bt[s]), preferred_element_type=F32) for s in range(SUBLANES)]
    y = pltpu.einshape("src->rsc", jnp.stack(ys, axis=0))
    o_ref[...] = x0_ref[...] * (y + v_ref[...] * bias)


def hyena_long_conv(v, x0, kspec, bias, n1, n2):
    b, l, width = v.shape
    nlen = n1 * n2
    half = n2 // 2
    ones = jnp.ones((1, width), F32)
    a = fft_stage1(v, 0, width, n1, n2, half, ones)
    mf = jnp.asarray(_cplx_dft_matrix(n1, -1.0), BF16)
    mi = jnp.asarray(_cplx_dft_matrix(n1, 1.0), BF16)
    twr, twi = _twiddle(n2, n1, 1.0)
    kb = _mid_block(n2)
    aspec = pl.BlockSpec((None, kb, n1, width), lambda ki_, bi: (bi, ki_, 0, 0))
    mspec = pl.BlockSpec((2 * n1, 2 * n1), lambda ki_, bi: (0, 0))
    tspec = pl.BlockSpec((kb, n1, 1), lambda ki_, bi: (ki_, 0, 0))
    bm = pl.pallas_call(
        functools.partial(_hyena_mid_kernel, n1=n1, kb=kb),
        out_shape=jax.ShapeDtypeStruct((b, n2, n1, width), U32),
        grid=(n2 // kb, b),
        in_specs=[aspec, pl.BlockSpec((kb, n1, width), lambda ki_, bi: (ki_, 0, 0)), mspec, mspec, tspec, tspec],
        out_specs=aspec,
        compiler_params=_params("parallel", "arbitrary"),
        name="hyena_mid",
    )(a, kspec, mf, mi, twr, twi)

    tc = _pick(width, (512, 256, 128))
    nct = width // tc
    cr, sr = _cos_sin(n2)
    finv = jnp.asarray(np.concatenate([cr, -sr], axis=1)[:half] / nlen, BF16)
    sv = (b, half, n1, width)
    sspec = pl.BlockSpec((None, half, SUBLANES, tc), lambda bi, ti, ci: (bi, 0, ti, ci))
    out = pl.pallas_call(
        _hyena_out_kernel,
        out_shape=jax.ShapeDtypeStruct(sv, F32),
        grid=(b, n1 // SUBLANES, nct),
        in_specs=[pl.BlockSpec((None, n2, SUBLANES, tc), lambda bi, ti, ci: (bi, 0, ti, ci)),
                  pl.BlockSpec((half, 2 * n2), lambda bi, ti, ci: (0, 0)),
                  sspec, sspec, pl.BlockSpec((1, tc), lambda bi, ti, ci: (0, ci))],
        out_specs=sspec,
        compiler_params=_params("parallel", "parallel", "arbitrary"),
        name="hyena_out",
    )(bm, finv, v.reshape(sv), x0.reshape(sv), bias.reshape(1, width))
    return out.reshape(b, l, width)


def _trunk(x, p, wb):
    b, l, d = x.shape
    m = b * l
    depth = p['mix_norm'].shape[0]
    fourier_w = d // 4
    attn_w = d - fourier_w
    n_heads = attn_w // HEAD_DIM
    kv_w = (n_heads // KV_RATIO) * HEAD_DIM
    conv_w = d // 2
    hy_w = d - conv_w
    slopes = jnp.asarray(alibi_slopes(n_heads))
    hn1, hn2 = _split_len(2 * l)

    xf = x.reshape(m, d)
    for layer in range(depth):
        i = layer // 2
        h = rms_norm(xf, p['mix_norm'][layer], BF16)
        if layer % 2 == 0:
            z = proj(h, wb['ab_w_in'], i, F32).reshape(b, l, -1)
            a = banded_attention(z, p['attn_sink'][i], slopes, n_heads)
            f = fourier_mix(z, attn_w + 2 * kv_w, fourier_w)
            parts = [a.reshape(m, attn_w), f.reshape(m, fourier_w)]
            xf = out_proj(parts, wb['ab_w_out'], i, xf)
        else:
            z = proj(h, wb['cd_w_in'], i, F32).reshape(b, l, -1)
            c = conformer_conv(z, conv_w, p['conv_dw_w'][i], p['conv_dw_b'][i], p['conv_ln_g'][i], p['conv_ln_b'][i])
            x0, v = hyena_gate(z, 2 * conv_w, hy_w, p['hy_short_w'][i], p['hy_short_b'][i])
            kern, norm = hyena_filter(l, hy_w, p['hy_filt_w1'][i], p['hy_filt_b1'][i], p['hy_filt_w2'][i],
                                      p['hy_filt_b2'][i], p['hy_filt_w3'][i], p['hy_filt_b3'][i],
                                      p['hy_filt_freq'][i], p['hy_filt_w4'][i])
            kspec = filter_spectrum(kern, norm, hn1, hn2)
            dd = hyena_long_conv(v, x0, kspec, p['hy_bias'][i], hn1, hn2)
            parts = [c.reshape(m, conv_w), dd.reshape(m, hy_w)]
            xf = out_proj(parts, wb['cd_w_out'], i, xf)
        h = rms_norm(xf, p['ffn_norm'][layer], BF16)
        hid = swiglu_in(h, wb['w_gate'], wb['w_up'], layer)
        xf = out_proj([hid], wb['w_down'], layer, xf)
    return rms_norm(xf, p['final_norm'], F32).reshape(b, l, d)


def kernel(x_prompt, x_sample, mix_norm, ffn_norm, final_norm, w_gate, w_up, w_down, ab_w_in, ab_w_out, attn_sink, cd_w_in, cd_w_out, conv_dw_w, conv_dw_b, conv_ln_g, conv_ln_b, hy_short_w, hy_short_b, hy_filt_w1, hy_filt_b1, hy_filt_w2, hy_filt_b2, hy_filt_w3, hy_filt_b3, hy_filt_freq, hy_filt_w4, hy_bias):
    p = dict(mix_norm=mix_norm, ffn_norm=ffn_norm, final_norm=final_norm, attn_sink=attn_sink,
             conv_dw_w=conv_dw_w, conv_dw_b=conv_dw_b, conv_ln_g=conv_ln_g, conv_ln_b=conv_ln_b,
             hy_short_w=hy_short_w, hy_short_b=hy_short_b, hy_filt_w1=hy_filt_w1, hy_filt_b1=hy_filt_b1,
             hy_filt_w2=hy_filt_w2, hy_filt_b2=hy_filt_b2, hy_filt_w3=hy_filt_w3, hy_filt_b3=hy_filt_b3,
             hy_filt_freq=hy_filt_freq, hy_filt_w4=hy_filt_w4, hy_bias=hy_bias)
    wb = dict(w_gate=w_gate.astype(BF16), w_up=w_up.astype(BF16), w_down=w_down.astype(BF16),
              ab_w_in=ab_w_in.astype(BF16), ab_w_out=ab_w_out.astype(BF16),
              cd_w_in=cd_w_in.astype(BF16), cd_w_out=cd_w_out.astype(BF16))
    return (_trunk(x_prompt, p, wb), _trunk(x_sample, p, wb))
```

```python
import functools
import math

import numpy as np
import jax
import jax.numpy as jnp
from jax import lax
from jax.experimental import pallas as pl
from jax.experimental.pallas import tpu as pltpu

F32 = jnp.float32
BF16 = jnp.bfloat16
U32 = jnp.uint32

HEAD_DIM = 128
KV_RATIO = 4
WINDOW = 128
BLOCK = 128
CONV_KERNEL = 31
CONV_HALO = 16
HYENA_SHORT = 3
HYENA_BANDS = 16
HYENA_MIN_DECAY = math.log(1e-2) / 1.5
HYENA_MAX_DECAY = math.log(1e-2) / 0.3
EPS = 1e-6
NEG_INF = -1e30

V7X_VMEM_BYTES = 64 * 1024 * 1024
VMEM_LIMIT = V7X_VMEM_BYTES - 8 * 1024 * 1024
LANES = 128
SUBLANES = 8


def _params(*sem):
    return pltpu.CompilerParams(dimension_semantics=sem, vmem_limit_bytes=VMEM_LIMIT)


def _pick(n, candidates):
    for c in candidates:
        if c <= n and n % c == 0:
            return c
    return n


def _rms_kernel(x_ref, g_ref, o_ref):
    x = x_ref[...]
    ms = jnp.mean(x * x, axis=-1, keepdims=True)
    o_ref[...] = (x * lax.rsqrt(ms + EPS) * g_ref[...]).astype(o_ref.dtype)


def rms_norm(x, g, out_dtype):
    m, d = x.shape
    tr = _pick(m, (256, 128, 64, 32, 16, 8))
    return pl.pallas_call(
        _rms_kernel,
        out_shape=jax.ShapeDtypeStruct((m, d), out_dtype),
        grid=(m // tr,),
        in_specs=[pl.BlockSpec((tr, d), lambda i: (i, 0)),
                  pl.BlockSpec((1, d), lambda i: (0, 0))],
        out_specs=pl.BlockSpec((tr, d), lambda i: (i, 0)),
        compiler_params=_params("parallel"),
        name="rms_norm",
    )(x, g.reshape(1, d))


def _prep_kernel(x_ref, xb_ref, r_ref):
    x = x_ref[...]
    xb_ref[...] = x.astype(BF16)
    r_ref[...] = lax.rsqrt(jnp.mean(x * x, axis=-1, keepdims=True) + EPS)


def prep(x):
    m, d = x.shape
    tr = _pick(m, (256, 128, 64, 32, 16, 8))
    return pl.pallas_call(
        _prep_kernel,
        out_shape=(jax.ShapeDtypeStruct((m, d), BF16), jax.ShapeDtypeStruct((m, 1), F32)),
        grid=(m // tr,),
        in_specs=[pl.BlockSpec((tr, d), lambda i: (i, 0))],
        out_specs=(pl.BlockSpec((tr, d), lambda i: (i, 0)), pl.BlockSpec((tr, 1), lambda i: (i, 0))),
        compiler_params=_params("parallel"),
        name="prep",
    )(x)


TILE_VMEM_BUDGET = VMEM_LIMIT - 12 * 1024 * 1024


def _row_spec(tm, k, single=False):
    if single:
        return pl.BlockSpec((tm, k), lambda i, j: (i, 0), pipeline_mode=pl.Buffered(1))
    return pl.BlockSpec((tm, k), lambda i, j: (i, 0))


def _scale_spec(tm):
    return pl.BlockSpec((tm, 1), lambda i, j: (i, 0))


def _proj_kernel(a_ref, r_ref, w_ref, o_ref):
    acc = jnp.dot(a_ref[...], w_ref[...], preferred_element_type=F32)
    o_ref[...] = (acc * r_ref[...]).astype(o_ref.dtype)


def proj(a, r, w, layer, out_dtype):
    m, k = a.shape
    n = w.shape[-1]
    osz = jnp.dtype(out_dtype).itemsize
    tm = _pick(m, (1024, 512, 256, 128))
    tn = next(t for t in (1024, 512, 256, 128)
              if n % t == 0 and 2 * tm * k * 2 + 2 * k * t * 2 + 2 * tm * t * osz <= TILE_VMEM_BUDGET)
    return pl.pallas_call(
        _proj_kernel,
        out_shape=jax.ShapeDtypeStruct((m, n), out_dtype),
        grid=(m // tm, n // tn),
        in_specs=[_row_spec(tm, k), _scale_spec(tm),
                  pl.BlockSpec((None, k, tn), lambda i, j: (layer, 0, j))],
        out_specs=pl.BlockSpec((tm, tn), lambda i, j: (i, j)),
        compiler_params=_params("parallel", "arbitrary"),
        name="proj",
    )(a, r, w)


def _swiglu_kernel(a_ref, r_ref, wg_ref, wu_ref, o_ref):
    a = a_ref[...]
    r = r_ref[...]
    g = jnp.dot(a, wg_ref[...], preferred_element_type=F32) * r
    u = jnp.dot(a, wu_ref[...], preferred_element_type=F32) * r
    o_ref[...] = (g * jax.nn.sigmoid(g) * u).astype(o_ref.dtype)


def swiglu_in(a, r, wg, wu, layer):
    m, k = a.shape
    n = wg.shape[-1]
    tm = _pick(m, (1024, 512, 256, 128))
    tn = _pick(n, (512, 256, 128))
    return pl.pallas_call(
        _swiglu_kernel,
        out_shape=jax.ShapeDtypeStruct((m, n), BF16),
        grid=(m // tm, n // tn),
        in_specs=[_row_spec(tm, k), _scale_spec(tm),
                  pl.BlockSpec((None, k, tn), lambda i, j: (layer, 0, j)),
                  pl.BlockSpec((None, k, tn), lambda i, j: (layer, 0, j))],
        out_specs=pl.BlockSpec((tm, tn), lambda i, j: (i, j)),
        compiler_params=_params("parallel", "arbitrary"),
        name="swiglu_in",
    )(a, r, wg, wu)


def _out_kernel(*refs, dtypes, d_model):
    n_parts = len(dtypes)
    a_refs = refs[:n_parts]
    w_refs = refs[n_parts:2 * n_parts]
    res_ref = refs[2 * n_parts]
    o_ref, ob_ref, r_ref, ss_ref = refs[2 * n_parts + 1:2 * n_parts + 5]
    cast_refs = list(refs[2 * n_parts + 5:])
    j = pl.program_id(1)

    @pl.when(j == 0)
    def _():
        ss_ref[...] = jnp.zeros_like(ss_ref)
        k = 0
        for a_ref, dt in zip(a_refs, dtypes):
            if dt != BF16:
                cast_refs[k][...] = a_ref[...].astype(BF16)
                k += 1

    acc = res_ref[...]
    k = 0
    for a_ref, w_ref, dt in zip(a_refs, w_refs, dtypes):
        if dt != BF16:
            a = cast_refs[k][...]
            k += 1
        else:
            a = a_ref[...]
        acc = acc + jnp.dot(a, w_ref[...], preferred_element_type=F32)
    o_ref[...] = acc
    ob_ref[...] = acc.astype(BF16)
    ss_ref[...] += jnp.sum(acc * acc, axis=-1, keepdims=True)

    @pl.when(j == pl.num_programs(1) - 1)
    def _():
        r_ref[...] = lax.rsqrt(ss_ref[...] * (1.0 / d_model) + EPS)


def out_proj(parts, w, layer, res):
    m = res.shape[0]
    n = w.shape[-1]
    kp = [p.shape[1] for p in parts]
    dtypes = tuple(p.dtype for p in parts)
    tm = _pick(m, (1024, 512, 256, 128))
    tile_bytes = sum(tm * k * p.dtype.itemsize for k, p in zip(kp, parts))
    cast_bytes = sum(tm * k * 2 for k, dt in zip(kp, dtypes) if dt != BF16)

    def fits(t, nbuf):
        return n % t == 0 and (nbuf * tile_bytes + cast_bytes + 2 * sum(kp) * t * 2
                               + 2 * tm * t * (4 + 4 + 2)) <= TILE_VMEM_BUDGET

    tn, nbuf = next((t, nb) for t in (1024, 512, 256, 128) for nb in (2, 1) if fits(t, nb))
    in_specs = [_row_spec(tm, k, single=nbuf == 1) for k in kp]
    off = 0
    for k in kp:
        assert off % k == 0, "each part must start at a multiple of its own width"
        blk = off // k
        in_specs.append(pl.BlockSpec((None, k, tn), lambda i, j, blk=blk: (layer, blk, j)))
        off += k
    tile = pl.BlockSpec((tm, tn), lambda i, j: (i, j))
    in_specs.append(tile)
    scratch = [pltpu.VMEM((tm, 1), F32)] + [pltpu.VMEM((tm, k), BF16) for k, dt in zip(kp, dtypes) if dt != BF16]
    return pl.pallas_call(
        functools.partial(_out_kernel, dtypes=dtypes, d_model=n),
        out_shape=(jax.ShapeDtypeStruct((m, n), F32), jax.ShapeDtypeStruct((m, n), BF16),
                   jax.ShapeDtypeStruct((m, 1), F32)),
        grid=(m // tm, n // tn),
        in_specs=in_specs,
        out_specs=(tile, tile, _scale_spec(tm)),
        scratch_shapes=scratch,
        compiler_params=_params("parallel", "arbitrary"),
        name="out_proj",
    )(*parts, *([w] * len(parts)), res)


def alibi_slopes(n):
    def pow2(m):
        start = 2.0 ** (-(2.0 ** -(math.log2(m) - 3)))
        return [start ** (i + 1) for i in range(m)]
    if math.log2(n).is_integer():
        s = pow2(n)
    else:
        c = 2 ** math.floor(math.log2(n))
        s = pow2(c) + pow2(2 * c)[0::2][: n - c]
    return np.asarray(s, np.float32)


def _attn_kernel(sink_ref, slope_ref, q_ref, kp_ref, kc_ref, kn_ref, vp_ref, vc_ref, vn_ref, o_ref, *, nsub):
    n = pl.program_id(1)
    g = pl.program_id(2)
    nb = pl.num_programs(1) * nsub
    k = jnp.concatenate([kp_ref[...], kc_ref[...], kn_ref[...]], axis=0).astype(BF16)
    v = jnp.concatenate([vp_ref[...], vc_ref[...], vn_ref[...]], axis=0).astype(BF16)
    row = lax.broadcasted_iota(jnp.int32, (BLOCK, 3 * BLOCK), 0)
    col = lax.broadcasted_iota(jnp.int32, (BLOCK, 3 * BLOCK), 1)
    dist = jnp.abs(col - BLOCK - row)
    band = dist <= WINDOW
    distf = dist.astype(F32)
    scale = HEAD_DIM ** -0.5
    heads = range(KV_RATIO)
    sinks = [sink_ref[g * KV_RATIO + r] for r in heads]
    bias = [jnp.where(band, -slope_ref[g * KV_RATIO + r] * distf, NEG_INF) for r in heads]
    for sub in range(nsub):
        blk = n * nsub + sub
        rows = slice(sub * BLOCK, (sub + 1) * BLOCK)
        keys = slice(sub * BLOCK, (sub + 3) * BLOCK)
        kb, vb = k[keys], v[keys]
        edge = None
        if sub == 0:
            edge = col >= jnp.where(blk > 0, 0, BLOCK)
        if sub == nsub - 1:
            hi = col < jnp.where(blk < nb - 1, 3 * BLOCK, 2 * BLOCK)
            edge = hi if edge is None else edge & hi
        s = [lax.dot_general(q_ref[rows, r * HEAD_DIM:(r + 1) * HEAD_DIM].astype(BF16), kb,
                             (((1,), (1,)), ((), ())), preferred_element_type=F32) * scale + bias[r] for r in heads]
        if edge is not None:
            s = [jnp.where(edge, sr, NEG_INF) for sr in s]
        mx = [jnp.maximum(jnp.max(s[r], axis=-1, keepdims=True), sinks[r]) for r in heads]
        p = [jnp.exp(s[r] - mx[r]) for r in heads]
        denom = [jnp.sum(p[r], axis=-1, keepdims=True) + jnp.exp(sinks[r] - mx[r]) for r in heads]
        o = [jnp.dot(p[r].astype(BF16), vb, preferred_element_type=F32) / denom[r] for r in heads]
        for r in heads:
            o_ref[rows, r * HEAD_DIM:(r + 1) * HEAD_DIM] = o[r].astype(o_ref.dtype)


def banded_attention(z, sink, slopes, n_heads):
    b, l, _ = z.shape
    g = n_heads // KV_RATIO
    tq = _pick(l, (512, 256, 128))
    nsub = tq // BLOCK
    nq = l // tq
    nb = l // BLOCK
    qw = KV_RATIO * HEAD_DIM
    kcol = n_heads
    vcol = n_heads + g

    def cur_spec(col0):
        return pl.BlockSpec((None, tq, HEAD_DIM), lambda bi, ni, gi: (bi, ni, col0 + gi))

    def halo_spec(col0, nxt):
        if nxt:
            return pl.BlockSpec((None, BLOCK, HEAD_DIM),
                                lambda bi, ni, gi: (bi, jnp.minimum((ni + 1) * nsub, nb - 1), col0 + gi))
        return pl.BlockSpec((None, BLOCK, HEAD_DIM),
                            lambda bi, ni, gi: (bi, jnp.maximum(ni * nsub - 1, 0), col0 + gi))

    smem = pl.BlockSpec(memory_space=pltpu.SMEM)
    return pl.pallas_call(
        functools.partial(_attn_kernel, nsub=nsub),
        out_shape=jax.ShapeDtypeStruct((b, l, n_heads * HEAD_DIM), BF16),
        grid=(b, nq, g),
        in_specs=[smem, smem,
                  pl.BlockSpec((None, tq, qw), lambda bi, ni, gi: (bi, ni, gi)),
                  halo_spec(kcol, False), cur_spec(kcol), halo_spec(kcol, True),
                  halo_spec(vcol, False), cur_spec(vcol), halo_spec(vcol, True)],
        out_specs=pl.BlockSpec((None, tq, qw), lambda bi, ni, gi: (bi, ni, gi)),
        compiler_params=_params("parallel", "parallel", "arbitrary"),
        name="banded_attention",
    )(sink.astype(F32), slopes, z, z, z, z, z, z, z)


def _split_len(n):
    lg = int(math.log2(n))
    assert 2 ** lg == n
    n2 = 2 ** ((lg + 1) // 2)
    return n // n2, n2


def _cos_sin(n):
    idx = np.arange(n)
    ang = 2.0 * np.pi * ((idx[:, None] * idx[None, :]) % n) / n
    return np.cos(ang), np.sin(ang)


def _twiddle(na, nb, sign):
    n = na * nb
    ang = 2.0 * np.pi * ((np.arange(na)[:, None] * np.arange(nb)[None, :]) % n) / n
    return (jnp.asarray(np.cos(ang)[:, :, None], F32), jnp.asarray(sign * np.sin(ang)[:, :, None], F32))


def _cplx_dft_matrix(n, sign, scale=1.0):
    c, s = _cos_sin(n)
    s = -sign * s
    return np.block([[c, s], [-s, c]]) * scale


def _pack_c(re, im):
    r = lax.bitcast_convert_type(re.astype(BF16).astype(F32), U32)
    i = lax.bitcast_convert_type(im.astype(BF16).astype(F32), U32)
    return r | (i >> 16)


def _unpack_c(p):
    re = lax.bitcast_convert_type(p & jnp.uint32(0xFFFF0000), F32)
    im = lax.bitcast_convert_type(p << 16, F32)
    return jnp.concatenate([re, im], axis=0).astype(BF16)


def _fft1_kernel(x_ref, s_ref, f_ref, twr_ref, twi_ref, a_ref, *, n2):
    scale = s_ref[...]
    f = f_ref[...]
    xt = pltpu.einshape("rsc->src", x_ref[...])
    outs = []
    for s in range(SUBLANES):
        x = (xt[s] * scale).astype(BF16)
        a = jnp.dot(f, x, preferred_element_type=F32)
        ar, ai = a[:n2], a[n2:]
        c, sn = twr_ref[s], twi_ref[s]
        outs.append(_pack_c(ar * c - ai * sn, ar * sn + ai * c))
    a_ref[...] = pltpu.einshape("src->rsc", jnp.stack(outs, axis=0))


def fft_stage1(x, col0, width, n1, n2, rows, scale):
    b, _, w = x.shape
    tc = _pick(math.gcd(math.gcd(w, col0) if col0 else w, width), (512, 256, 128))
    nct = width // tc
    c0 = col0 // tc
    cr, sr = _cos_sin(n2)
    f = jnp.asarray(np.concatenate([cr, -sr], axis=0)[:, :rows], BF16)
    twr, twi = _twiddle(n1, n2, -1.0)
    tspec = pl.BlockSpec((SUBLANES, n2, 1), lambda bi, ni, ci: (ni, 0, 0))
    return pl.pallas_call(
        functools.partial(_fft1_kernel, n2=n2),
        out_shape=jax.ShapeDtypeStruct((b, n2, n1, width), U32),
        grid=(b, n1 // SUBLANES, nct),
        in_specs=[pl.BlockSpec((None, rows, SUBLANES, tc), lambda bi, ni, ci: (bi, 0, ni, c0 + ci)),
                  pl.BlockSpec((1, tc), lambda bi, ni, ci: (0, ci)),
                  pl.BlockSpec((2 * n2, rows), lambda bi, ni, ci: (0, 0)),
                  tspec, tspec],
        out_specs=pl.BlockSpec((None, n2, SUBLANES, tc), lambda bi, ni, ci: (bi, 0, ni, ci)),
        compiler_params=_params("parallel", "parallel", "arbitrary"),
        name="fft_stage1",
    )(x.reshape(b, rows, n1, w), scale, f, twr, twi)


def _fourier2_kernel(a_ref, m_ref, cs_ref, o_ref, *, n1, groups):
    outs = []
    for s in range(SUBLANES):
        gm = jnp.dot(m_ref[...], _unpack_c(a_ref[s]), preferred_element_type=F32)
        gr, gi = gm[:n1].astype(BF16), gm[n1:].astype(BF16)
        cols = []
        for q in range(groups):
            sl = slice(q * HEAD_DIM, (q + 1) * HEAD_DIM)
            lhs = jnp.concatenate([gr[:, sl], gi[:, sl]], axis=1)
            cols.append(jnp.dot(lhs, cs_ref[...], preferred_element_type=F32))
        outs.append(jnp.concatenate(cols, axis=1))
    o_ref[...] = pltpu.einshape("src->rsc", jnp.stack(outs, axis=0))


def fourier_mix(z, col0, width):
    b, l, _ = z.shape
    n1, n2 = _split_len(l)
    ones = jnp.ones((1, width), F32)
    a = fft_stage1(z, col0, width, n1, n2, n2, ones)
    m = jnp.asarray(_cplx_dft_matrix(n1, -1.0), BF16)
    cc, sc = _cos_sin(HEAD_DIM)
    cs = jnp.asarray(np.concatenate([cc, sc], axis=0) / math.sqrt(l * HEAD_DIM), BF16)
    groups = width // HEAD_DIM
    out = pl.pallas_call(
        functools.partial(_fourier2_kernel, n1=n1, groups=groups),
        out_shape=jax.ShapeDtypeStruct((b, n1, n2, width), F32),
        grid=(b, n2 // SUBLANES),
        in_specs=[pl.BlockSpec((None, SUBLANES, n1, width), lambda bi, ki: (bi, ki, 0, 0)),
                  pl.BlockSpec((2 * n1, 2 * n1), lambda bi, ki: (0, 0)),
                  pl.BlockSpec((2 * HEAD_DIM, HEAD_DIM), lambda bi, ki: (0, 0))],
        out_specs=pl.BlockSpec((None, n1, SUBLANES, width), lambda bi, ki: (bi, 0, ki, 0)),
        compiler_params=_params("parallel", "arbitrary"),
        name="fourier_stage2",
    )(a, m, cs)
    return out.reshape(b, l, width)


def _conformer_kernel(ap_ref, ac_ref, an_ref, gp_ref, gc_ref, gn_ref, w_ref, b_ref, lg_ref, lb_ref,
                      o_ref, buf_ref, sh_ref, conv_ref, *, tl, width):
    t = pl.program_id(1)
    nt = pl.num_programs(1)
    h = CONV_HALO
    prev = ap_ref[...] * jax.nn.sigmoid(gp_ref[...])
    nxt = an_ref[...] * jax.nn.sigmoid(gn_ref[...])
    buf_ref[0:h, :] = jnp.where(t > 0, prev, 0.0)
    buf_ref[h:h + tl, :] = ac_ref[...] * jax.nn.sigmoid(gc_ref[...])
    buf_ref[h + tl:h + tl + h, :] = jnp.where(t < nt - 1, nxt, 0.0)
    span = tl + 2 * h - SUBLANES
    for ph in range(SUBLANES):
        sh_ref[ph, 0:span, :] = buf_ref[ph:ph + span, :]
    base = h - CONV_KERNEL // 2
    for c in range(width // LANES):
        sl = slice(c * LANES, (c + 1) * LANES)
        acc = jnp.broadcast_to(b_ref[:, sl], (SUBLANES, LANES))[None]
        for j in range(CONV_KERNEL):
            a8, ph = divmod(base + j, SUBLANES)
            wj = jnp.broadcast_to(w_ref[j:j + 1, sl], (SUBLANES, LANES))[None]
            xs = sh_ref[ph, a8 * SUBLANES:a8 * SUBLANES + tl, sl].reshape(tl // SUBLANES, SUBLANES, LANES)
            acc = acc + wj * xs
        conv_ref[:, sl] = acc.reshape(tl, LANES)
    y = conv_ref[...]
    mu = jnp.mean(y, axis=-1, keepdims=True)
    yc = y - mu
    var = jnp.mean(yc * yc, axis=-1, keepdims=True)
    yn = yc * lax.rsqrt(var + EPS) * lg_ref[...] + lb_ref[...]
    o_ref[...] = (yn * jax.nn.sigmoid(yn)).astype(o_ref.dtype)


def conformer_conv(z, width, dw_w, dw_b, ln_g, ln_b):
    b, l, _ = z.shape
    tl = _pick(l, (128, 64, 32, 16))
    h = CONV_HALO
    r = tl // h
    nh = l // h

    def cur(col):
        return pl.BlockSpec((None, tl, width), lambda bi, ti: (bi, ti, col))

    def halo(col, nxt):
        if nxt:
            return pl.BlockSpec((None, h, width), lambda bi, ti: (bi, jnp.minimum((ti + 1) * r, nh - 1), col))
        return pl.BlockSpec((None, h, width), lambda bi, ti: (bi, jnp.maximum(ti * r - 1, 0), col))

    vec = pl.BlockSpec((1, width), lambda bi, ti: (0, 0))
    return pl.pallas_call(
        functools.partial(_conformer_kernel, tl=tl, width=width),
        out_shape=jax.ShapeDtypeStruct((b, l, width), BF16),
        grid=(b, l // tl),
        in_specs=[halo(0, False), cur(0), halo(0, True), halo(1, False), cur(1), halo(1, True),
                  pl.BlockSpec((CONV_KERNEL, width), lambda bi, ti: (0, 0)), vec, vec, vec],
        out_specs=pl.BlockSpec((None, tl, width), lambda bi, ti: (bi, ti, 0)),
        scratch_shapes=[pltpu.VMEM((tl + 2 * h, width), F32),
                        pltpu.VMEM((SUBLANES, tl + 2 * h, width), F32),
                        pltpu.VMEM((tl, width), F32)],
        compiler_params=_params("parallel", "arbitrary"),
        name="conformer_conv",
    )(z, z, z, z, z, z, dw_w, dw_b.reshape(1, width), ln_g.reshape(1, width), ln_b.reshape(1, width))


def _short_conv(prev_ref, cur_ref, next_ref, w_ref, b_ref, first, last, tl):
    x = cur_ref[...]
    rows = lax.broadcasted_iota(jnp.int32, x.shape, 0)
    pr = jnp.where(first, 0.0, prev_ref[SUBLANES - 1:SUBLANES, :])
    nx = jnp.where(last, 0.0, next_ref[0:1, :])
    xm = jnp.where(rows == 0, pr, pltpu.roll(x, 1, axis=0))
    xp = jnp.where(rows == tl - 1, nx, pltpu.roll(x, tl - 1, axis=0))
    return w_ref[0:1, :] * xm + w_ref[1:2, :] * x + w_ref[2:3, :] * xp + b_ref[...]


def _hyena_gate_kernel(*refs, tl):
    x0 = refs[0:3]
    x1 = refs[3:6]
    xv = refs[6:9]
    w0, w1, wv, b0, b1, bv, x0_out, v_out = refs[9:]
    t = pl.program_id(1)
    first = t == 0
    last = t == pl.num_programs(1) - 1
    x0_out[...] = _short_conv(*x0, w0, b0, first, last, tl)
    v_out[...] = _short_conv(*xv, wv, bv, first, last, tl) * _short_conv(*x1, w1, b1, first, last, tl)


def hyena_gate(z, col0, width, sw, sb):
    b, l, _ = z.shape
    tl = _pick(l, (512, 256, 128, 64, 32, 16, 8))
    tc = _pick(math.gcd(col0, width), (512, 256, 128))
    r = tl // SUBLANES
    nh = l // SUBLANES
    nct = width // tc

    def trio(part):
        c0 = (col0 + part * width) // tc
        return [pl.BlockSpec((None, SUBLANES, tc), lambda bi, ti, ci: (bi, jnp.maximum(ti * r - 1, 0), c0 + ci)),
                pl.BlockSpec((None, tl, tc), lambda bi, ti, ci: (bi, ti, c0 + ci)),
                pl.BlockSpec((None, SUBLANES, tc), lambda bi, ti, ci: (bi, jnp.minimum((ti + 1) * r, nh - 1), c0 + ci))]

    def wspec(part, rows):
        return pl.BlockSpec((rows, tc), lambda bi, ti, ci: (0, part * nct + ci))

    out = jax.ShapeDtypeStruct((b, l, width), F32)
    ospec = pl.BlockSpec((None, tl, tc), lambda bi, ti, ci: (bi, ti, ci))
    sb2 = sb.reshape(1, 3 * width)
    return pl.pallas_call(
        functools.partial(_hyena_gate_kernel, tl=tl),
        out_shape=(out, out),
        grid=(b, l // tl, nct),
        in_specs=trio(0) + trio(1) + trio(2) + [wspec(0, HYENA_SHORT), wspec(1, HYENA_SHORT), wspec(2, HYENA_SHORT),
                                                 wspec(0, 1), wspec(1, 1), wspec(2, 1)],
        out_specs=(ospec, ospec),
        compiler_params=_params("parallel", "parallel", "arbitrary"),
        name="hyena_gate",
    )(*([z] * 9), sw, sw, sw, sb2, sb2, sb2)


def _hdot(a, b):
    return jnp.dot(a, b, preferred_element_type=F32, precision=lax.Precision.HIGHEST)


def _filter_kernel(bands_ref, w1t_ref, w1c_ref, w1s_ref, b1_ref, w2_ref, b2_ref, w3_ref, b3_ref, fr_ref,
                   w4_ref, dec_ref, k_ref, norm_ref, *, l, tl):
    i = pl.program_id(0)
    m = i * tl + lax.broadcasted_iota(jnp.int32, (tl, 1), 0)
    j = jnp.where(m < l, m, 2 * l - m).astype(F32)
    t = j / (l - 1.0)
    ang = (2.0 * math.pi / l) * bands_ref[...] * j
    fr = fr_ref[...]
    pre = t * w1t_ref[...] + _hdot(jnp.cos(ang), w1c_ref[...]) - _hdot(jnp.sin(ang), w1s_ref[...])
    hcur = jnp.sin(fr * (pre + b1_ref[...]))
    hcur = jnp.sin(fr * (_hdot(hcur, w2_ref[...]) + b2_ref[...]))
    hcur = jnp.sin(fr * (_hdot(hcur, w3_ref[...]) + b3_ref[...]))
    out = jnp.dot(hcur.astype(BF16), w4_ref[...].astype(BF16), preferred_element_type=F32) * jnp.exp(-t * dec_ref[...])
    out = jnp.where(m == l, 0.0, out)
    k_ref[...] = out

    @pl.when(i == 0)
    def _():
        norm_ref[...] = jnp.zeros_like(norm_ref)

    norm_ref[...] += jnp.sum(jnp.abs(out), axis=0, keepdims=True)


def hyena_filter(l, width, w1, b1, w2, b2, w3, b3, freq, w4):
    hid = w1.shape[1]
    tl = _pick(l, (512, 256, 128, 64, 32, 16, 8))
    nl = l // tl
    bands = jnp.asarray(np.linspace(1e-4, HYENA_BANDS - 1, HYENA_BANDS, dtype=np.float32)[None, :])
    dec = jnp.asarray(np.abs(np.linspace(HYENA_MIN_DECAY, HYENA_MAX_DECAY, width, dtype=np.float32))[None, :])
    full = lambda shape: pl.BlockSpec(shape, lambda i: tuple(0 for _ in shape))
    w4v = w4.reshape(hid, 2, width).transpose(1, 0, 2)
    return pl.pallas_call(
        functools.partial(_filter_kernel, l=l, tl=tl),
        out_shape=(jax.ShapeDtypeStruct((2 * l, width), F32), jax.ShapeDtypeStruct((1, width), F32)),
        grid=(2 * nl,),
        in_specs=[full((1, HYENA_BANDS)), full((1, hid)), full((HYENA_BANDS, hid)), full((HYENA_BANDS, hid)),
                  full((1, hid)), full((hid, hid)), full((1, hid)), full((hid, hid)), full((1, hid)), full((1, hid)),
                  pl.BlockSpec((None, hid, width), lambda i: (i // nl, 0, 0)),
                  full((1, width))],
        out_specs=(pl.BlockSpec((tl, width), lambda i: (i, 0)), full((1, width))),
        compiler_params=_params("arbitrary"),
        name="hyena_filter",
    )(bands, w1[0:1], w1[1:1 + HYENA_BANDS], w1[1 + HYENA_BANDS:], b1.reshape(1, hid), w2, b2.reshape(1, hid),
      w3, b3.reshape(1, hid), freq.reshape(1, hid), w4v, dec)


def _spectrum2_kernel(a_ref, m_ref, k_ref, *, n1, kb):
    for s in range(kb):
        k = jnp.dot(m_ref[...], _unpack_c(a_ref[s]), preferred_element_type=F32)
        k_ref[s] = _pack_c(k[:n1], k[n1:])


def _mid_block(n2):
    return 2 if n2 % 2 == 0 else 1


def filter_spectrum(kern, norm, n1, n2):
    nlen, width = kern.shape
    a = fft_stage1(kern.reshape(1, nlen, width), 0, width, n1, n2, n2, 1.0 / norm)
    m = jnp.asarray(_cplx_dft_matrix(n1, -1.0), BF16)
    kb = _mid_block(n2)
    spec = pl.BlockSpec((kb, n1, width), lambda ki: (ki, 0, 0))
    return pl.pallas_call(
        functools.partial(_spectrum2_kernel, n1=n1, kb=kb),
        out_shape=jax.ShapeDtypeStruct((n2, n1, width), U32),
        grid=(n2 // kb,),
        in_specs=[spec, pl.BlockSpec((2 * n1, 2 * n1), lambda ki: (0, 0))],
        out_specs=spec,
        compiler_params=_params("arbitrary"),
        name="filter_spectrum",
    )(a.reshape(n2, n1, width), m)


def _hyena_mid_kernel(a_ref, k_ref, mf_ref, mi_ref, twr_ref, twi_ref, b_ref, *, n1, kb):
    for s in range(kb):
        v = jnp.dot(mf_ref[...], _unpack_c(a_ref[s]), preferred_element_type=F32)
        vr, vi = v[:n1], v[n1:]
        kp = k_ref[s]
        kr = lax.bitcast_convert_type(kp & jnp.uint32(0xFFFF0000), F32)
        ki = lax.bitcast_convert_type(kp << 16, F32)
        y = jnp.concatenate([vr * kr - vi * ki, vr * ki + vi * kr], axis=0).astype(BF16)
        bm = jnp.dot(mi_ref[...], y, preferred_element_type=F32)
        br, bi = bm[:n1], bm[n1:]
        c, sn = twr_ref[s], twi_ref[s]
        b_ref[s] = _pack_c(br * c - bi * sn, br * sn + bi * c)


def _hyena_out_kernel(b_ref, f_ref, v_ref, x0_ref, bias_ref, o_ref):
    bias = bias_ref[...]
    f = f_ref[...]
    bt = pltpu.einshape("rsc->src", b_ref[...])
    ys = [jnp.dot(f, _unpack_c(bt[s]), preferred_element_type=F32) for s in range(SUBLANES)]
    y = pltpu.einshape("src->rsc", jnp.stack(ys, axis=0))
    o_ref[...] = x0_ref[...] * (y + v_ref[...] * bias)


def hyena_long_conv(v, x0, kspec, bias, n1, n2):
    b, l, width = v.shape
    nlen = n1 * n2
    half = n2 // 2
    ones = jnp.ones((1, width), F32)
    a = fft_stage1(v, 0, width, n1, n2, half, ones)
    mf = jnp.asarray(_cplx_dft_matrix(n1, -1.0), BF16)
    mi = jnp.asarray(_cplx_dft_matrix(n1, 1.0), BF16)
    twr, twi = _twiddle(n2, n1, 1.0)
    kb = _mid_block(n2)
    aspec = pl.BlockSpec((None, kb, n1, width), lambda ki_, bi: (bi, ki_, 0, 0))
    mspec = pl.BlockSpec((2 * n1, 2 * n1), lambda ki_, bi: (0, 0))
    tspec = pl.BlockSpec((kb, n1, 1), lambda ki_, bi: (ki_, 0, 0))
    bm = pl.pallas_call(
        functools.partial(_hyena_mid_kernel, n1=n1, kb=kb),
        out_shape=jax.ShapeDtypeStruct((b, n2, n1, width), U32),
        grid=(n2 // kb, b),
        in_specs=[aspec, pl.BlockSpec((kb, n1, width), lambda ki_, bi: (ki_, 0, 0)), mspec, mspec, tspec, tspec],
        out_specs=aspec,
        compiler_params=_params("parallel", "arbitrary"),
        name="hyena_mid",
    )(a, kspec, mf, mi, twr, twi)

    tc = _pick(width, (512, 256, 128))
    nct = width // tc
    cr, sr = _cos_sin(n2)
    finv = jnp.asarray(np.concatenate([cr, -sr], axis=1)[:half] / nlen, BF16)
    sv = (b, half, n1, width)
    sspec = pl.BlockSpec((None, half, SUBLANES, tc), lambda bi, ti, ci: (bi, 0, ti, ci))
    out = pl.pallas_call(
        _hyena_out_kernel,
        out_shape=jax.ShapeDtypeStruct(sv, F32),
        grid=(b, n1 // SUBLANES, nct),
        in_specs=[pl.BlockSpec((None, n2, SUBLANES, tc), lambda bi, ti, ci: (bi, 0, ti, ci)),
                  pl.BlockSpec((half, 2 * n2), lambda bi, ti, ci: (0, 0)),
                  sspec, sspec, pl.BlockSpec((1, tc), lambda bi, ti, ci: (0, ci))],
        out_specs=sspec,
        compiler_params=_params("parallel", "parallel", "arbitrary"),
        name="hyena_out",
    )(bm, finv, v.reshape(sv), x0.reshape(sv), bias.reshape(1, width))
    return out.reshape(b, l, width)


def _trunk(x, p, wb):
    b, l, d = x.shape
    m = b * l
    depth = p['mix_norm'].shape[0]
    fourier_w = d // 4
    attn_w = d - fourier_w
    n_heads = attn_w // HEAD_DIM
    kv_w = (n_heads // KV_RATIO) * HEAD_DIM
    conv_w = d // 2
    hy_w = d - conv_w
    slopes = jnp.asarray(alibi_slopes(n_heads))
    hn1, hn2 = _split_len(2 * l)

    xf = x.reshape(m, d)
    xb, r = prep(xf)
    for layer in range(depth):
        i = layer // 2
        if layer % 2 == 0:
            z = proj(xb, r, wb['ab_w_in'], i, F32).reshape(b, l, -1)
            a = banded_attention(z, p['attn_sink'][i], slopes, n_heads)
            f = fourier_mix(z, attn_w + 2 * kv_w, fourier_w)
            parts = [a.reshape(m, attn_w), f.reshape(m, fourier_w)]
            xf, xb, r = out_proj(parts, wb['ab_w_out'], i, xf)
        else:
            z = proj(xb, r, wb['cd_w_in'], i, F32).reshape(b, l, -1)
            c = conformer_conv(z, conv_w, p['conv_dw_w'][i], p['conv_dw_b'][i], p['conv_ln_g'][i], p['conv_ln_b'][i])
            x0, v = hyena_gate(z, 2 * conv_w, hy_w, p['hy_short_w'][i], p['hy_short_b'][i])
            kern, norm = hyena_filter(l, hy_w, p['hy_filt_w1'][i], p['hy_filt_b1'][i], p['hy_filt_w2'][i],
                                      p['hy_filt_b2'][i], p['hy_filt_w3'][i], p['hy_filt_b3'][i],
                                      p['hy_filt_freq'][i], p['hy_filt_w4'][i])
            kspec = filter_spectrum(kern, norm, hn1, hn2)
            dd = hyena_long_conv(v, x0, kspec, p['hy_bias'][i], hn1, hn2)
            parts = [c.reshape(m, conv_w), dd.reshape(m, hy_w)]
            xf, xb, r = out_proj(parts, wb['cd_w_out'], i, xf)
        hid = swiglu_in(xb, r, wb['w_gate'], wb['w_up'], layer)
        xf, xb, r = out_proj([hid], wb['w_down'], layer, xf)
    return rms_norm(xf, p['final_norm'], F32).reshape(b, l, d)


def kernel(x_prompt, x_sample, mix_norm, ffn_norm, final_norm, w_gate, w_up, w_down, ab_w_in, ab_w_out, attn_sink, cd_w_in, cd_w_out, conv_dw_w, conv_dw_b, conv_ln_g, conv_ln_b, hy_short_w, hy_short_b, hy_filt_w1, hy_filt_b1, hy_filt_w2, hy_filt_b2, hy_filt_w3, hy_filt_b3, hy_filt_freq, hy_filt_w4, hy_bias):
    p = dict(mix_norm=mix_norm, ffn_norm=ffn_norm, final_norm=final_norm, attn_sink=attn_sink,
             conv_dw_w=conv_dw_w, conv_dw_b=conv_dw_b, conv_ln_g=conv_ln_g, conv_ln_b=conv_ln_b,
             hy_short_w=hy_short_w, hy_short_b=hy_short_b, hy_filt_w1=hy_filt_w1, hy_filt_b1=hy_filt_b1,
             hy_filt_w2=hy_filt_w2, hy_filt_b2=hy_filt_b2, hy_filt_w3=hy_filt_w3, hy_filt_b3=hy_filt_b3,
             hy_filt_freq=hy_filt_freq, hy_filt_w4=hy_filt_w4, hy_bias=hy_bias)
    g_ab, g_cd, g_ffn = mix_norm[0::2, :, None], mix_norm[1::2, :, None], ffn_norm[:, :, None]
    wb = dict(w_gate=(g_ffn * w_gate).astype(BF16), w_up=(g_ffn * w_up).astype(BF16), w_down=w_down.astype(BF16),
              ab_w_in=(g_ab * ab_w_in).astype(BF16), ab_w_out=ab_w_out.astype(BF16),
              cd_w_in=(g_cd * cd_w_in).astype(BF16), cd_w_out=cd_w_out.astype(BF16))
    return (_trunk(x_prompt, p, wb), _trunk(x_sample, p, wb))
```

```python
import functools
import math

import numpy as np
import jax
import jax.numpy as jnp
from jax import lax
from jax.experimental import pallas as pl
from jax.experimental.pallas import tpu as pltpu

F32 = jnp.float32
BF16 = jnp.bfloat16
U32 = jnp.uint32

HEAD_DIM = 128
KV_RATIO = 4
WINDOW = 128
BLOCK = 128
CONV_KERNEL = 31
CONV_HALO = 16
HYENA_SHORT = 3
HYENA_BANDS = 16
HYENA_MIN_DECAY = math.log(1e-2) / 1.5
HYENA_MAX_DECAY = math.log(1e-2) / 0.3
EPS = 1e-6
NEG_INF = -1e30

V7X_VMEM_BYTES = 64 * 1024 * 1024
VMEM_LIMIT = V7X_VMEM_BYTES - 8 * 1024 * 1024
LANES = 128
SUBLANES = 8


def _params(*sem):
    return pltpu.CompilerParams(dimension_semantics=sem, vmem_limit_bytes=VMEM_LIMIT)


def _pick(n, candidates):
    for c in candidates:
        if c <= n and n % c == 0:
            return c
    return n


def _rms_kernel(x_ref, g_ref, o_ref):
    x = x_ref[...]
    ms = jnp.mean(x * x, axis=-1, keepdims=True)
    o_ref[...] = (x * lax.rsqrt(ms + EPS) * g_ref[...]).astype(o_ref.dtype)


def rms_norm(x, g, out_dtype):
    m, d = x.shape
    tr = _pick(m, (256, 128, 64, 32, 16, 8))
    return pl.pallas_call(
        _rms_kernel,
        out_shape=jax.ShapeDtypeStruct((m, d), out_dtype),
        grid=(m // tr,),
        in_specs=[pl.BlockSpec((tr, d), lambda i: (i, 0)),
                  pl.BlockSpec((1, d), lambda i: (0, 0))],
        out_specs=pl.BlockSpec((tr, d), lambda i: (i, 0)),
        compiler_params=_params("parallel"),
        name="rms_norm",
    )(x, g.reshape(1, d))


def _prep_kernel(x_ref, xb_ref, r_ref):
    x = x_ref[...]
    xb_ref[...] = x.astype(BF16)
    r_ref[...] = lax.rsqrt(jnp.mean(x * x, axis=-1, keepdims=True) + EPS)


def prep(x):
    m, d = x.shape
    tr = _pick(m, (256, 128, 64, 32, 16, 8))
    return pl.pallas_call(
        _prep_kernel,
        out_shape=(jax.ShapeDtypeStruct((m, d), BF16), jax.ShapeDtypeStruct((m, 1), F32)),
        grid=(m // tr,),
        in_specs=[pl.BlockSpec((tr, d), lambda i: (i, 0))],
        out_specs=(pl.BlockSpec((tr, d), lambda i: (i, 0)), pl.BlockSpec((tr, 1), lambda i: (i, 0))),
        compiler_params=_params("parallel"),
        name="prep",
    )(x)


TILE_VMEM_BUDGET = VMEM_LIMIT - 12 * 1024 * 1024


def _row_spec(tm, k, single=False):
    if single:
        return pl.BlockSpec((tm, k), lambda i, j: (i, 0), pipeline_mode=pl.Buffered(1))
    return pl.BlockSpec((tm, k), lambda i, j: (i, 0))


def _scale_spec(tm):
    return pl.BlockSpec((tm, 1), lambda i, j: (i, 0))


def _proj_kernel(a_ref, r_ref, w_ref, o_ref):
    acc = jnp.dot(a_ref[...], w_ref[...], preferred_element_type=F32)
    o_ref[...] = (acc * r_ref[...]).astype(o_ref.dtype)


def proj(a, r, w, layer, out_dtype):
    m, k = a.shape
    n = w.shape[-1]
    osz = jnp.dtype(out_dtype).itemsize
    tm = _pick(m, (1024, 512, 256, 128))
    tn = next(t for t in (1024, 512, 256, 128)
              if n % t == 0 and 2 * tm * k * 2 + 2 * k * t * 2 + 2 * tm * t * osz <= TILE_VMEM_BUDGET)
    return pl.pallas_call(
        _proj_kernel,
        out_shape=jax.ShapeDtypeStruct((m, n), out_dtype),
        grid=(m // tm, n // tn),
        in_specs=[_row_spec(tm, k), _scale_spec(tm),
                  pl.BlockSpec((None, k, tn), lambda i, j: (layer, 0, j))],
        out_specs=pl.BlockSpec((tm, tn), lambda i, j: (i, j)),
        compiler_params=_params("parallel", "arbitrary"),
        name="proj",
    )(a, r, w)


def _swiglu_kernel(a_ref, r_ref, wg_ref, wu_ref, o_ref):
    a = a_ref[...]
    r = r_ref[...]
    g = jnp.dot(a, wg_ref[...], preferred_element_type=F32) * r
    u = jnp.dot(a, wu_ref[...], preferred_element_type=F32) * r
    o_ref[...] = (g * jax.nn.sigmoid(g) * u).astype(o_ref.dtype)


def swiglu_in(a, r, wg, wu, layer):
    m, k = a.shape
    n = wg.shape[-1]
    tm = _pick(m, (1024, 512, 256, 128))
    tn = _pick(n, (512, 256, 128))
    return pl.pallas_call(
        _swiglu_kernel,
        out_shape=jax.ShapeDtypeStruct((m, n), BF16),
        grid=(m // tm, n // tn),
        in_specs=[_row_spec(tm, k), _scale_spec(tm),
                  pl.BlockSpec((None, k, tn), lambda i, j: (layer, 0, j)),
                  pl.BlockSpec((None, k, tn), lambda i, j: (layer, 0, j))],
        out_specs=pl.BlockSpec((tm, tn), lambda i, j: (i, j)),
        compiler_params=_params("parallel", "arbitrary"),
        name="swiglu_in",
    )(a, r, wg, wu)


def _out_kernel(*refs, dtypes, d_model, emit_norm):
    n_parts = len(dtypes)
    a_refs = refs[:n_parts]
    w_refs = refs[n_parts:2 * n_parts]
    res_ref = refs[2 * n_parts]
    if emit_norm:
        o_ref, ob_ref, r_ref, ss_ref = refs[2 * n_parts + 1:2 * n_parts + 5]
        cast_refs = list(refs[2 * n_parts + 5:])
    else:
        o_ref = refs[2 * n_parts + 1]
        cast_refs = list(refs[2 * n_parts + 2:])
    j = pl.program_id(1)

    @pl.when(j == 0)
    def _():
        if emit_norm:
            ss_ref[...] = jnp.zeros_like(ss_ref)
        k = 0
        for a_ref, dt in zip(a_refs, dtypes):
            if dt != BF16:
                cast_refs[k][...] = a_ref[...].astype(BF16)
                k += 1

    acc = res_ref[...]
    k = 0
    for a_ref, w_ref, dt in zip(a_refs, w_refs, dtypes):
        if dt != BF16:
            a = cast_refs[k][...]
            k += 1
        else:
            a = a_ref[...]
        acc = acc + jnp.dot(a, w_ref[...], preferred_element_type=F32)
    o_ref[...] = acc
    if emit_norm:
        ob_ref[...] = acc.astype(BF16)
        ss_ref[...] += jnp.sum(acc * acc, axis=-1, keepdims=True)

        @pl.when(j == pl.num_programs(1) - 1)
        def _():
            r_ref[...] = lax.rsqrt(ss_ref[...] * (1.0 / d_model) + EPS)


def out_proj(parts, w, layer, res, k0=0, emit_norm=True):
    parts = [p if isinstance(p, tuple) else (p, 0, p.shape[1]) for p in parts]
    m = res.shape[0]
    n = w.shape[-1]
    kp = [k for _, _, k in parts]
    dtypes = tuple(a.dtype for a, _, _ in parts)
    tm = _pick(m, (1024, 512, 256, 128))
    tile_bytes = sum(tm * k * a.dtype.itemsize for a, _, k in parts)
    cast_bytes = sum(tm * k * 2 for k, dt in zip(kp, dtypes) if dt != BF16)
    out_bytes = 4 + 4 + 2 if emit_norm else 4 + 4

    def fits(t, nbuf):
        return n % t == 0 and (nbuf * tile_bytes + cast_bytes + 2 * sum(kp) * t * 2
                               + 2 * tm * t * out_bytes) <= TILE_VMEM_BUDGET

    tn, nbuf = next((t, nb) for t in (1024, 512, 256, 128) for nb in (2, 1) if fits(t, nb))
    in_specs = []
    for _, cb, k in parts:
        if nbuf == 1:
            in_specs.append(pl.BlockSpec((tm, k), lambda i, j, cb=cb: (i, cb), pipeline_mode=pl.Buffered(1)))
        else:
            in_specs.append(pl.BlockSpec((tm, k), lambda i, j, cb=cb: (i, cb)))
    off = k0
    for k in kp:
        assert off % k == 0, "each part must start at a multiple of its own width"
        blk = off // k
        in_specs.append(pl.BlockSpec((None, k, tn), lambda i, j, blk=blk: (layer, blk, j)))
        off += k
    tile = pl.BlockSpec((tm, tn), lambda i, j: (i, j))
    in_specs.append(tile)
    scratch = [pltpu.VMEM((tm, k), BF16) for k, dt in zip(kp, dtypes) if dt != BF16]
    if emit_norm:
        out_shape = (jax.ShapeDtypeStruct((m, n), F32), jax.ShapeDtypeStruct((m, n), BF16),
                     jax.ShapeDtypeStruct((m, 1), F32))
        out_specs = (tile, tile, _scale_spec(tm))
        scratch = [pltpu.VMEM((tm, 1), F32)] + scratch
    else:
        out_shape = jax.ShapeDtypeStruct((m, n), F32)
        out_specs = tile
    return pl.pallas_call(
        functools.partial(_out_kernel, dtypes=dtypes, d_model=n, emit_norm=emit_norm),
        out_shape=out_shape,
        grid=(m // tm, n // tn),
        in_specs=in_specs,
        out_specs=out_specs,
        scratch_shapes=scratch,
        compiler_params=_params("parallel", "arbitrary"),
        name="out_proj",
    )(*[a for a, _, _ in parts], *([w] * len(parts)), res)


def alibi_slopes(n):
    def pow2(m):
        start = 2.0 ** (-(2.0 ** -(math.log2(m) - 3)))
        return [start ** (i + 1) for i in range(m)]
    if math.log2(n).is_integer():
        s = pow2(n)
    else:
        c = 2 ** math.floor(math.log2(n))
        s = pow2(c) + pow2(2 * c)[0::2][: n - c]
    return np.asarray(s, np.float32)


def _attn_kernel(sink_ref, slope_ref, q_ref, kp_ref, kc_ref, kn_ref, vp_ref, vc_ref, vn_ref, o_ref, *, nsub):
    n = pl.program_id(1)
    g = pl.program_id(2)
    nb = pl.num_programs(1) * nsub
    k = jnp.concatenate([kp_ref[...], kc_ref[...], kn_ref[...]], axis=0).astype(BF16)
    v = jnp.concatenate([vp_ref[...], vc_ref[...], vn_ref[...]], axis=0).astype(BF16)
    row = lax.broadcasted_iota(jnp.int32, (BLOCK, 3 * BLOCK), 0)
    col = lax.broadcasted_iota(jnp.int32, (BLOCK, 3 * BLOCK), 1)
    dist = jnp.abs(col - BLOCK - row)
    band = dist <= WINDOW
    distf = dist.astype(F32)
    scale = HEAD_DIM ** -0.5
    heads = range(KV_RATIO)
    sinks = [sink_ref[g * KV_RATIO + r] for r in heads]
    bias = [jnp.where(band, -slope_ref[g * KV_RATIO + r] * distf, NEG_INF) for r in heads]
    for sub in range(nsub):
        blk = n * nsub + sub
        rows = slice(sub * BLOCK, (sub + 1) * BLOCK)
        keys = slice(sub * BLOCK, (sub + 3) * BLOCK)
        kb, vb = k[keys], v[keys]
        edge = None
        if sub == 0:
            edge = col >= jnp.where(blk > 0, 0, BLOCK)
        if sub == nsub - 1:
            hi = col < jnp.where(blk < nb - 1, 3 * BLOCK, 2 * BLOCK)
            edge = hi if edge is None else edge & hi
        s = [lax.dot_general(q_ref[rows, r * HEAD_DIM:(r + 1) * HEAD_DIM].astype(BF16), kb,
                             (((1,), (1,)), ((), ())), preferred_element_type=F32) * scale + bias[r] for r in heads]
        if edge is not None:
            s = [jnp.where(edge, sr, NEG_INF) for sr in s]
        mx = [jnp.maximum(jnp.max(s[r], axis=-1, keepdims=True), sinks[r]) for r in heads]
        p = [jnp.exp(s[r] - mx[r]) for r in heads]
        denom = [jnp.sum(p[r], axis=-1, keepdims=True) + jnp.exp(sinks[r] - mx[r]) for r in heads]
        o = [jnp.dot(p[r].astype(BF16), vb, preferred_element_type=F32) / denom[r] for r in heads]
        for r in heads:
            o_ref[rows, r * HEAD_DIM:(r + 1) * HEAD_DIM] = o[r].astype(o_ref.dtype)


def banded_attention(z, sink, slopes, n_heads):
    b, l, _ = z.shape
    g = n_heads // KV_RATIO
    tq = _pick(l, (512, 256, 128))
    nsub = tq // BLOCK
    nq = l // tq
    nb = l // BLOCK
    qw = KV_RATIO * HEAD_DIM
    kcol = n_heads
    vcol = n_heads + g

    def cur_spec(col0):
        return pl.BlockSpec((None, tq, HEAD_DIM), lambda bi, ni, gi: (bi, ni, col0 + gi))

    def halo_spec(col0, nxt):
        if nxt:
            return pl.BlockSpec((None, BLOCK, HEAD_DIM),
                                lambda bi, ni, gi: (bi, jnp.minimum((ni + 1) * nsub, nb - 1), col0 + gi))
        return pl.BlockSpec((None, BLOCK, HEAD_DIM),
                            lambda bi, ni, gi: (bi, jnp.maximum(ni * nsub - 1, 0), col0 + gi))

    smem = pl.BlockSpec(memory_space=pltpu.SMEM)
    return pl.pallas_call(
        functools.partial(_attn_kernel, nsub=nsub),
        out_shape=jax.ShapeDtypeStruct((b, l, n_heads * HEAD_DIM), BF16),
        grid=(b, nq, g),
        in_specs=[smem, smem,
                  pl.BlockSpec((None, tq, qw), lambda bi, ni, gi: (bi, ni, gi)),
                  halo_spec(kcol, False), cur_spec(kcol), halo_spec(kcol, True),
                  halo_spec(vcol, False), cur_spec(vcol), halo_spec(vcol, True)],
        out_specs=pl.BlockSpec((None, tq, qw), lambda bi, ni, gi: (bi, ni, gi)),
        compiler_params=_params("parallel", "parallel", "arbitrary"),
        name="banded_attention",
    )(sink.astype(F32), slopes, z, z, z, z, z, z, z)


def _split_len(n):
    lg = int(math.log2(n))
    assert 2 ** lg == n
    n2 = 2 ** ((lg + 1) // 2)
    return n // n2, n2


def _cos_sin(n):
    idx = np.arange(n)
    ang = 2.0 * np.pi * ((idx[:, None] * idx[None, :]) % n) / n
    return np.cos(ang), np.sin(ang)


def _twiddle(na, nb, sign):
    n = na * nb
    ang = 2.0 * np.pi * ((np.arange(na)[:, None] * np.arange(nb)[None, :]) % n) / n
    return (jnp.asarray(np.cos(ang)[:, :, None], F32), jnp.asarray(sign * np.sin(ang)[:, :, None], F32))


def _cplx_dft_matrix(n, sign, scale=1.0):
    c, s = _cos_sin(n)
    s = -sign * s
    return np.block([[c, s], [-s, c]]) * scale


def _pack_c(re, im):
    r = lax.bitcast_convert_type(re.astype(BF16).astype(F32), U32)
    i = lax.bitcast_convert_type(im.astype(BF16).astype(F32), U32)
    return r | (i >> 16)


def _unpack_c(p):
    re = lax.bitcast_convert_type(p & jnp.uint32(0xFFFF0000), F32)
    im = lax.bitcast_convert_type(p << 16, F32)
    return jnp.concatenate([re, im], axis=0).astype(BF16)


def _fft1_kernel(x_ref, s_ref, f_ref, twr_ref, twi_ref, a_ref, *, n2):
    scale = s_ref[...]
    f = f_ref[...]
    xt = pltpu.einshape("rsc->src", x_ref[...])
    outs = []
    for s in range(SUBLANES):
        x = (xt[s] * scale).astype(BF16)
        a = jnp.dot(f, x, preferred_element_type=F32)
        ar, ai = a[:n2], a[n2:]
        c, sn = twr_ref[s], twi_ref[s]
        outs.append(_pack_c(ar * c - ai * sn, ar * sn + ai * c))
    a_ref[...] = pltpu.einshape("src->rsc", jnp.stack(outs, axis=0))


def fft_stage1(x, col0, width, n1, n2, rows, scale, kout=None):
    b, _, w = x.shape
    kout = n2 if kout is None else kout
    tc = _pick(math.gcd(math.gcd(w, col0) if col0 else w, width), (512, 256, 128))
    nct = width // tc
    c0 = col0 // tc
    cr, sr = _cos_sin(n2)
    f = jnp.asarray(np.concatenate([cr[:kout], -sr[:kout]], axis=0)[:, :rows], BF16)
    twr, twi = _twiddle(n1, n2, -1.0)
    twr, twi = twr[:, :kout], twi[:, :kout]
    tspec = pl.BlockSpec((SUBLANES, kout, 1), lambda bi, ni, ci: (ni, 0, 0))
    return pl.pallas_call(
        functools.partial(_fft1_kernel, n2=kout),
        out_shape=jax.ShapeDtypeStruct((b, kout, n1, width), U32),
        grid=(b, n1 // SUBLANES, nct),
        in_specs=[pl.BlockSpec((None, rows, SUBLANES, tc), lambda bi, ni, ci: (bi, 0, ni, c0 + ci)),
                  pl.BlockSpec((1, tc), lambda bi, ni, ci: (0, ci)),
                  pl.BlockSpec((2 * kout, rows), lambda bi, ni, ci: (0, 0)),
                  tspec, tspec],
        out_specs=pl.BlockSpec((None, kout, SUBLANES, tc), lambda bi, ni, ci: (bi, 0, ni, ci)),
        compiler_params=_params("parallel", "parallel", "arbitrary"),
        name="fft_stage1",
    )(x.reshape(b, rows, n1, w), scale, f, twr, twi)


def _fourier2_kernel(a_ref, m_ref, cs_ref, o_ref, *, n1, groups):
    outs = []
    for s in range(SUBLANES):
        gm = jnp.dot(m_ref[...], _unpack_c(a_ref[s]), preferred_element_type=F32)
        gr, gi = gm[:n1].astype(BF16), gm[n1:].astype(BF16)
        cols = []
        for q in range(groups):
            sl = slice(q * HEAD_DIM, (q + 1) * HEAD_DIM)
            lhs = jnp.concatenate([gr[:, sl], gi[:, sl]], axis=1)
            cols.append(jnp.dot(lhs, cs_ref[...], preferred_element_type=F32))
        outs.append(jnp.concatenate(cols, axis=1))
    o_ref[...] = pltpu.einshape("src->rsc", jnp.stack(outs, axis=0))


def fourier_mix(z, col0, width):
    b, l, _ = z.shape
    n1, n2 = _split_len(l)
    ones = jnp.ones((1, width), F32)
    a = fft_stage1(z, col0, width, n1, n2, n2, ones)
    m = jnp.asarray(_cplx_dft_matrix(n1, -1.0), BF16)
    cc, sc = _cos_sin(HEAD_DIM)
    cs = jnp.asarray(np.concatenate([cc, sc], axis=0) / math.sqrt(l * HEAD_DIM), BF16)
    groups = width // HEAD_DIM
    out = pl.pallas_call(
        functools.partial(_fourier2_kernel, n1=n1, groups=groups),
        out_shape=jax.ShapeDtypeStruct((b, n1, n2, width), F32),
        grid=(b, n2 // SUBLANES),
        in_specs=[pl.BlockSpec((None, SUBLANES, n1, width), lambda bi, ki: (bi, ki, 0, 0)),
                  pl.BlockSpec((2 * n1, 2 * n1), lambda bi, ki: (0, 0)),
                  pl.BlockSpec((2 * HEAD_DIM, HEAD_DIM), lambda bi, ki: (0, 0))],
        out_specs=pl.BlockSpec((None, n1, SUBLANES, width), lambda bi, ki: (bi, 0, ki, 0)),
        compiler_params=_params("parallel", "arbitrary"),
        name="fourier_stage2",
    )(a, m, cs)
    return out.reshape(b, l, width)


def _conformer_kernel(ap_ref, ac_ref, an_ref, gp_ref, gc_ref, gn_ref, w_ref, b_ref, lg_ref, lb_ref,
                      o_ref, buf_ref, sh_ref, conv_ref, *, tl, width):
    t = pl.program_id(1)
    nt = pl.num_programs(1)
    h = CONV_HALO
    prev = ap_ref[...] * jax.nn.sigmoid(gp_ref[...])
    nxt = an_ref[...] * jax.nn.sigmoid(gn_ref[...])
    buf_ref[0:h, :] = jnp.where(t > 0, prev, 0.0)
    buf_ref[h:h + tl, :] = ac_ref[...] * jax.nn.sigmoid(gc_ref[...])
    buf_ref[h + tl:h + tl + h, :] = jnp.where(t < nt - 1, nxt, 0.0)
    span = tl + 2 * h - SUBLANES
    for ph in range(SUBLANES):
        sh_ref[ph, 0:span, :] = buf_ref[ph:ph + span, :]
    base = h - CONV_KERNEL // 2
    for c in range(width // LANES):
        sl = slice(c * LANES, (c + 1) * LANES)
        acc = jnp.broadcast_to(b_ref[:, sl], (SUBLANES, LANES))[None]
        for j in range(CONV_KERNEL):
            a8, ph = divmod(base + j, SUBLANES)
            wj = jnp.broadcast_to(w_ref[j:j + 1, sl], (SUBLANES, LANES))[None]
            xs = sh_ref[ph, a8 * SUBLANES:a8 * SUBLANES + tl, sl].reshape(tl // SUBLANES, SUBLANES, LANES)
            acc = acc + wj * xs
        conv_ref[:, sl] = acc.reshape(tl, LANES)
    y = conv_ref[...]
    mu = jnp.mean(y, axis=-1, keepdims=True)
    yc = y - mu
    var = jnp.mean(yc * yc, axis=-1, keepdims=True)
    yn = yc * lax.rsqrt(var + EPS) * lg_ref[...] + lb_ref[...]
    o_ref[...] = (yn * jax.nn.sigmoid(yn)).astype(o_ref.dtype)


def conformer_conv(z, width, dw_w, dw_b, ln_g, ln_b):
    b, l, _ = z.shape
    tl = _pick(l, (128, 64, 32, 16))
    h = CONV_HALO
    r = tl // h
    nh = l // h

    def cur(col):
        return pl.BlockSpec((None, tl, width), lambda bi, ti: (bi, ti, col))

    def halo(col, nxt):
        if nxt:
            return pl.BlockSpec((None, h, width), lambda bi, ti: (bi, jnp.minimum((ti + 1) * r, nh - 1), col))
        return pl.BlockSpec((None, h, width), lambda bi, ti: (bi, jnp.maximum(ti * r - 1, 0), col))

    vec = pl.BlockSpec((1, width), lambda bi, ti: (0, 0))
    return pl.pallas_call(
        functools.partial(_conformer_kernel, tl=tl, width=width),
        out_shape=jax.ShapeDtypeStruct((b, l, width), BF16),
        grid=(b, l // tl),
        in_specs=[halo(0, False), cur(0), halo(0, True), halo(1, False), cur(1), halo(1, True),
                  pl.BlockSpec((CONV_KERNEL, width), lambda bi, ti: (0, 0)), vec, vec, vec],
        out_specs=pl.BlockSpec((None, tl, width), lambda bi, ti: (bi, ti, 0)),
        scratch_shapes=[pltpu.VMEM((tl + 2 * h, width), F32),
                        pltpu.VMEM((SUBLANES, tl + 2 * h, width), F32),
                        pltpu.VMEM((tl, width), F32)],
        compiler_params=_params("parallel", "arbitrary"),
        name="conformer_conv",
    )(z, z, z, z, z, z, dw_w, dw_b.reshape(1, width), ln_g.reshape(1, width), ln_b.reshape(1, width))


def _short_conv(prev_ref, cur_ref, next_ref, w_ref, b_ref, first, last, tl):
    x = cur_ref[...]
    rows = lax.broadcasted_iota(jnp.int32, x.shape, 0)
    pr = jnp.where(first, 0.0, prev_ref[SUBLANES - 1:SUBLANES, :])
    nx = jnp.where(last, 0.0, next_ref[0:1, :])
    xm = jnp.where(rows == 0, pr, pltpu.roll(x, 1, axis=0))
    xp = jnp.where(rows == tl - 1, nx, pltpu.roll(x, tl - 1, axis=0))
    return w_ref[0:1, :] * xm + w_ref[1:2, :] * x + w_ref[2:3, :] * xp + b_ref[...]


def _hyena_gate_kernel(*refs, tl):
    x0 = refs[0:3]
    x1 = refs[3:6]
    xv = refs[6:9]
    w0, w1, wv, b0, b1, bv, x0_out, v_out = refs[9:]
    t = pl.program_id(1)
    first = t == 0
    last = t == pl.num_programs(1) - 1
    x0_out[...] = _short_conv(*x0, w0, b0, first, last, tl)
    v_out[...] = _short_conv(*xv, wv, bv, first, last, tl) * _short_conv(*x1, w1, b1, first, last, tl)


def hyena_gate(z, col0, width, sw, sb):
    b, l, _ = z.shape
    tl = _pick(l, (512, 256, 128, 64, 32, 16, 8))
    tc = _pick(math.gcd(col0, width), (512, 256, 128))
    r = tl // SUBLANES
    nh = l // SUBLANES
    nct = width // tc

    def trio(part):
        c0 = (col0 + part * width) // tc
        return [pl.BlockSpec((None, SUBLANES, tc), lambda bi, ti, ci: (bi, jnp.maximum(ti * r - 1, 0), c0 + ci)),
                pl.BlockSpec((None, tl, tc), lambda bi, ti, ci: (bi, ti, c0 + ci)),
                pl.BlockSpec((None, SUBLANES, tc), lambda bi, ti, ci: (bi, jnp.minimum((ti + 1) * r, nh - 1), c0 + ci))]

    def wspec(part, rows):
        return pl.BlockSpec((rows, tc), lambda bi, ti, ci: (0, part * nct + ci))

    out = jax.ShapeDtypeStruct((b, l, width), F32)
    ospec = pl.BlockSpec((None, tl, tc), lambda bi, ti, ci: (bi, ti, ci))
    sb2 = sb.reshape(1, 3 * width)
    return pl.pallas_call(
        functools.partial(_hyena_gate_kernel, tl=tl),
        out_shape=(out, out),
        grid=(b, l // tl, nct),
        in_specs=trio(0) + trio(1) + trio(2) + [wspec(0, HYENA_SHORT), wspec(1, HYENA_SHORT), wspec(2, HYENA_SHORT),
                                                 wspec(0, 1), wspec(1, 1), wspec(2, 1)],
        out_specs=(ospec, ospec),
        compiler_params=_params("parallel", "parallel", "arbitrary"),
        name="hyena_gate",
    )(*([z] * 9), sw, sw, sw, sb2, sb2, sb2)


def _hdot(a, b):
    return jnp.dot(a, b, preferred_element_type=F32, precision=lax.Precision.HIGHEST)


def _filter_kernel(bands_ref, w1t_ref, w1c_ref, w1s_ref, b1_ref, w2_ref, b2_ref, w3_ref, b3_ref, fr_ref,
                   w4_ref, dec_ref, k_ref, norm_ref, *, l, tl):
    i = pl.program_id(0)
    m = i * tl + lax.broadcasted_iota(jnp.int32, (tl, 1), 0)
    j = jnp.where(m < l, m, 2 * l - m).astype(F32)
    t = j / (l - 1.0)
    ang = (2.0 * math.pi / l) * bands_ref[...] * j
    fr = fr_ref[...]
    pre = t * w1t_ref[...] + _hdot(jnp.cos(ang), w1c_ref[...]) - _hdot(jnp.sin(ang), w1s_ref[...])
    hcur = jnp.sin(fr * (pre + b1_ref[...]))
    hcur = jnp.sin(fr * (_hdot(hcur, w2_ref[...]) + b2_ref[...]))
    hcur = jnp.sin(fr * (_hdot(hcur, w3_ref[...]) + b3_ref[...]))
    out = jnp.dot(hcur.astype(BF16), w4_ref[...].astype(BF16), preferred_element_type=F32) * jnp.exp(-t * dec_ref[...])
    out = jnp.where(m == l, 0.0, out)
    k_ref[...] = out

    @pl.when(i == 0)
    def _():
        norm_ref[...] = jnp.zeros_like(norm_ref)

    norm_ref[...] += jnp.sum(jnp.abs(out), axis=0, keepdims=True)


def hyena_filter(l, width, w1, b1, w2, b2, w3, b3, freq, w4):
    hid = w1.shape[1]
    tl = _pick(l, (512, 256, 128, 64, 32, 16, 8))
    nl = l // tl
    bands = jnp.asarray(np.linspace(1e-4, HYENA_BANDS - 1, HYENA_BANDS, dtype=np.float32)[None, :])
    dec = jnp.asarray(np.abs(np.linspace(HYENA_MIN_DECAY, HYENA_MAX_DECAY, width, dtype=np.float32))[None, :])
    full = lambda shape: pl.BlockSpec(shape, lambda i: tuple(0 for _ in shape))
    w4v = w4.reshape(hid, 2, width).transpose(1, 0, 2)
    return pl.pallas_call(
        functools.partial(_filter_kernel, l=l, tl=tl),
        out_shape=(jax.ShapeDtypeStruct((2 * l, width), F32), jax.ShapeDtypeStruct((1, width), F32)),
        grid=(2 * nl,),
        in_specs=[full((1, HYENA_BANDS)), full((1, hid)), full((HYENA_BANDS, hid)), full((HYENA_BANDS, hid)),
                  full((1, hid)), full((hid, hid)), full((1, hid)), full((hid, hid)), full((1, hid)), full((1, hid)),
                  pl.BlockSpec((None, hid, width), lambda i: (i // nl, 0, 0)),
                  full((1, width))],
        out_specs=(pl.BlockSpec((tl, width), lambda i: (i, 0)), full((1, width))),
        compiler_params=_params("arbitrary"),
        name="hyena_filter",
    )(bands, w1[0:1], w1[1:1 + HYENA_BANDS], w1[1 + HYENA_BANDS:], b1.reshape(1, hid), w2, b2.reshape(1, hid),
      w3, b3.reshape(1, hid), freq.reshape(1, hid), w4v, dec)


def _spectrum2_kernel(a_ref, m_ref, k_ref, *, n1, kb):
    for s in range(kb):
        k = jnp.dot(m_ref[...], _unpack_c(a_ref[s]), preferred_element_type=F32)
        k_ref[s] = _pack_c(k[:n1], k[n1:])


MID_BLOCK = 2


def _half_residues(n2):
    return n2 // 2 + SUBLANES


def filter_spectrum(kern, norm, n1, n2):
    nlen, width = kern.shape
    n2h = _half_residues(n2)
    a = fft_stage1(kern.reshape(1, nlen, width), 0, width, n1, n2, n2, 1.0 / norm, kout=n2h)
    m = jnp.asarray(_cplx_dft_matrix(n1, -1.0), BF16)
    kb = MID_BLOCK
    spec = pl.BlockSpec((kb, n1, width), lambda ki: (ki, 0, 0))
    return pl.pallas_call(
        functools.partial(_spectrum2_kernel, n1=n1, kb=kb),
        out_shape=jax.ShapeDtypeStruct((n2h, n1, width), U32),
        grid=(n2h // kb,),
        in_specs=[spec, pl.BlockSpec((2 * n1, 2 * n1), lambda ki: (0, 0))],
        out_specs=spec,
        compiler_params=_params("arbitrary"),
        name="filter_spectrum",
    )(a.reshape(n2h, n1, width), m)


def _hyena_mid_kernel(a_ref, k_ref, mf_ref, mi_ref, twr_ref, twi_ref, b_ref, *, n1, kb):
    for s in range(kb):
        v = jnp.dot(mf_ref[...], _unpack_c(a_ref[s]), preferred_element_type=F32)
        vr, vi = v[:n1], v[n1:]
        kp = k_ref[s]
        kr = lax.bitcast_convert_type(kp & jnp.uint32(0xFFFF0000), F32)
        ki = lax.bitcast_convert_type(kp << 16, F32)
        y = jnp.concatenate([vr * kr - vi * ki, vr * ki + vi * kr], axis=0).astype(BF16)
        bm = jnp.dot(mi_ref[...], y, preferred_element_type=F32)
        br, bi = bm[:n1], bm[n1:]
        c, sn = twr_ref[s], twi_ref[s]
        b_ref[s] = _pack_c(br * c - bi * sn, br * sn + bi * c)


def _hyena_out_kernel(b_ref, f_ref, v_ref, x0_ref, bias_ref, o_ref):
    bias = bias_ref[...]
    f = f_ref[...]
    bt = pltpu.einshape("rsc->src", b_ref[...])
    ys = [jnp.dot(f, _unpack_c(bt[s]), preferred_element_type=F32) for s in range(SUBLANES)]
    y = pltpu.einshape("src->rsc", jnp.stack(ys, axis=0))
    o_ref[...] = x0_ref[...] * (y + v_ref[...] * bias)


def hyena_long_conv(v, x0, kspec, bias, n1, n2):
    b, l, width = v.shape
    nlen = n1 * n2
    half = n2 // 2
    n2h = _half_residues(n2)
    ones = jnp.ones((1, width), F32)
    a = fft_stage1(v, 0, width, n1, n2, half, ones, kout=n2h)
    mf = jnp.asarray(_cplx_dft_matrix(n1, -1.0), BF16)
    mi = jnp.asarray(_cplx_dft_matrix(n1, 1.0), BF16)
    twr, twi = _twiddle(n2, n1, 1.0)
    twr, twi = twr[:n2h], twi[:n2h]
    kb = MID_BLOCK
    aspec = pl.BlockSpec((None, kb, n1, width), lambda ki_, bi: (bi, ki_, 0, 0))
    mspec = pl.BlockSpec((2 * n1, 2 * n1), lambda ki_, bi: (0, 0))
    tspec = pl.BlockSpec((kb, n1, 1), lambda ki_, bi: (ki_, 0, 0))
    bm = pl.pallas_call(
        functools.partial(_hyena_mid_kernel, n1=n1, kb=kb),
        out_shape=jax.ShapeDtypeStruct((b, n2h, n1, width), U32),
        grid=(n2h // kb, b),
        in_specs=[aspec, pl.BlockSpec((kb, n1, width), lambda ki_, bi: (ki_, 0, 0)), mspec, mspec, tspec, tspec],
        out_specs=aspec,
        compiler_params=_params("parallel", "arbitrary"),
        name="hyena_mid",
    )(a, kspec, mf, mi, twr, twi)

    tc = _pick(width, (512, 256, 128))
    nct = width // tc
    cr, sr = _cos_sin(n2)
    coef = np.where(np.arange(n2h) > half, 0.0, np.where((np.arange(n2h) == 0) | (np.arange(n2h) == half), 1.0, 2.0))
    finv = jnp.asarray(np.concatenate([cr[:half, :n2h] * coef, -sr[:half, :n2h] * coef], axis=1) / nlen, BF16)
    sv = (b, half, n1, width)
    sspec = pl.BlockSpec((None, half, SUBLANES, tc), lambda bi, ti, ci: (bi, 0, ti, ci))
    out = pl.pallas_call(
        _hyena_out_kernel,
        out_shape=jax.ShapeDtypeStruct(sv, F32),
        grid=(b, n1 // SUBLANES, nct),
        in_specs=[pl.BlockSpec((None, n2h, SUBLANES, tc), lambda bi, ti, ci: (bi, 0, ti, ci)),
                  pl.BlockSpec((half, 2 * n2h), lambda bi, ti, ci: (0, 0)),
                  sspec, sspec, pl.BlockSpec((1, tc), lambda bi, ti, ci: (0, ci))],
        out_specs=sspec,
        compiler_params=_params("parallel", "parallel", "arbitrary"),
        name="hyena_out",
    )(bm, finv, v.reshape(sv), x0.reshape(sv), bias.reshape(1, width))
    return out.reshape(b, l, width)


def _trunk(x, p, wb):
    b, l, d = x.shape
    m = b * l
    depth = p['mix_norm'].shape[0]
    fourier_w = d // 4
    attn_w = d - fourier_w
    n_heads = attn_w // HEAD_DIM
    kv_w = (n_heads // KV_RATIO) * HEAD_DIM
    conv_w = d // 2
    hy_w = d - conv_w
    slopes = jnp.asarray(alibi_slopes(n_heads))
    hn1, hn2 = _split_len(2 * l)

    xf = x.reshape(m, d)
    xb, r = prep(xf)
    for layer in range(depth):
        i = layer // 2
        if layer % 2 == 0:
            z = proj(xb, r, wb['ab_w_in'], i, F32).reshape(b, l, -1)
            a = banded_attention(z, p['attn_sink'][i], slopes, n_heads)
            f = fourier_mix(z, attn_w + 2 * kv_w, fourier_w)
            parts = [a.reshape(m, attn_w), f.reshape(m, fourier_w)]
            xf, xb, r = out_proj(parts, wb['ab_w_out'], i, xf)
        else:
            z = proj(xb, r, wb['cd_w_in'], i, F32).reshape(b, l, -1)
            c = conformer_conv(z, conv_w, p['conv_dw_w'][i], p['conv_dw_b'][i], p['conv_ln_g'][i], p['conv_ln_b'][i])
            x0, v = hyena_gate(z, 2 * conv_w, hy_w, p['hy_short_w'][i], p['hy_short_b'][i])
            kern, norm = hyena_filter(l, hy_w, p['hy_filt_w1'][i], p['hy_filt_b1'][i], p['hy_filt_w2'][i],
                                      p['hy_filt_b2'][i], p['hy_filt_w3'][i], p['hy_filt_b3'][i],
                                      p['hy_filt_freq'][i], p['hy_filt_w4'][i])
            kspec = filter_spectrum(kern, norm, hn1, hn2)
            dd = hyena_long_conv(v, x0, kspec, p['hy_bias'][i], hn1, hn2)
            parts = [c.reshape(m, conv_w), dd.reshape(m, hy_w)]
            xf, xb, r = out_proj(parts, wb['cd_w_out'], i, xf)
        hid = swiglu_in(xb, r, wb['w_gate'], wb['w_up'], layer)
        kh = hid.shape[1] // 2
        assert kh % LANES == 0
        xf = out_proj([(hid, 0, kh)], wb['w_down'], layer, xf, emit_norm=False)
        xf, xb, r = out_proj([(hid, 1, kh)], wb['w_down'], layer, xf, k0=kh)
    return rms_norm(xf, p['final_norm'], F32).reshape(b, l, d)


def kernel(x_prompt, x_sample, mix_norm, ffn_norm, final_norm, w_gate, w_up, w_down, ab_w_in, ab_w_out, attn_sink, cd_w_in, cd_w_out, conv_dw_w, conv_dw_b, conv_ln_g, conv_ln_b, hy_short_w, hy_short_b, hy_filt_w1, hy_filt_b1, hy_filt_w2, hy_filt_b2, hy_filt_w3, hy_filt_b3, hy_filt_freq, hy_filt_w4, hy_bias):
    p = dict(mix_norm=mix_norm, ffn_norm=ffn_norm, final_norm=final_norm, attn_sink=attn_sink,
             conv_dw_w=conv_dw_w, conv_dw_b=conv_dw_b, conv_ln_g=conv_ln_g, conv_ln_b=conv_ln_b,
             hy_short_w=hy_short_w, hy_short_b=hy_short_b, hy_filt_w1=hy_filt_w1, hy_filt_b1=hy_filt_b1,
             hy_filt_w2=hy_filt_w2, hy_filt_b2=hy_filt_b2, hy_filt_w3=hy_filt_w3, hy_filt_b3=hy_filt_b3,
             hy_filt_freq=hy_filt_freq, hy_filt_w4=hy_filt_w4, hy_bias=hy_bias)
    g_ab, g_cd, g_ffn = mix_norm[0::2, :, None], mix_norm[1::2, :, None], ffn_norm[:, :, None]
    wb = dict(w_gate=(g_ffn * w_gate).astype(BF16), w_up=(g_ffn * w_up).astype(BF16), w_down=w_down.astype(BF16),
              ab_w_in=(g_ab * ab_w_in).astype(BF16), ab_w_out=ab_w_out.astype(BF16),
              cd_w_in=(g_cd * cd_w_in).astype(BF16), cd_w_out=cd_w_out.astype(BF16))
    return (_trunk(x_prompt, p, wb), _trunk(x_sample, p, wb))
```

```python
import functools
import math

import numpy as np
import jax
import jax.numpy as jnp
from jax import lax
from jax.experimental import pallas as pl
from jax.experimental.pallas import tpu as pltpu

F32 = jnp.float32
BF16 = jnp.bfloat16
U32 = jnp.uint32

HEAD_DIM = 128
KV_RATIO = 4
WINDOW = 128
BLOCK = 128
CONV_KERNEL = 31
CONV_HALO = 16
HYENA_SHORT = 3
HYENA_BANDS = 16
HYENA_MIN_DECAY = math.log(1e-2) / 1.5
HYENA_MAX_DECAY = math.log(1e-2) / 0.3
EPS = 1e-6
NEG_INF = -1e30

V7X_VMEM_BYTES = 64 * 1024 * 1024
VMEM_LIMIT = V7X_VMEM_BYTES - 8 * 1024 * 1024
LANES = 128
SUBLANES = 8


def _params(*sem):
    return pltpu.CompilerParams(dimension_semantics=sem, vmem_limit_bytes=VMEM_LIMIT)


def _pick(n, candidates):
    for c in candidates:
        if c <= n and n % c == 0:
            return c
    return n


def _rms_kernel(x_ref, g_ref, o_ref):
    x = x_ref[...]
    ms = jnp.mean(x * x, axis=-1, keepdims=True)
    o_ref[...] = (x * lax.rsqrt(ms + EPS) * g_ref[...]).astype(o_ref.dtype)


def rms_norm(x, g, out_dtype):
    m, d = x.shape
    tr = _pick(m, (256, 128, 64, 32, 16, 8))
    return pl.pallas_call(
        _rms_kernel,
        out_shape=jax.ShapeDtypeStruct((m, d), out_dtype),
        grid=(m // tr,),
        in_specs=[pl.BlockSpec((tr, d), lambda i: (i, 0)),
                  pl.BlockSpec((1, d), lambda i: (0, 0))],
        out_specs=pl.BlockSpec((tr, d), lambda i: (i, 0)),
        compiler_params=_params("parallel"),
        name="rms_norm",
    )(x, g.reshape(1, d))


def _prep_kernel(x_ref, xb_ref, r_ref):
    x = x_ref[...]
    xb_ref[...] = x.astype(BF16)
    r_ref[...] = lax.rsqrt(jnp.mean(x * x, axis=-1, keepdims=True) + EPS)


def prep(x):
    m, d = x.shape
    tr = _pick(m, (256, 128, 64, 32, 16, 8))
    return pl.pallas_call(
        _prep_kernel,
        out_shape=(jax.ShapeDtypeStruct((m, d), BF16), jax.ShapeDtypeStruct((m, 1), F32)),
        grid=(m // tr,),
        in_specs=[pl.BlockSpec((tr, d), lambda i: (i, 0))],
        out_specs=(pl.BlockSpec((tr, d), lambda i: (i, 0)), pl.BlockSpec((tr, 1), lambda i: (i, 0))),
        compiler_params=_params("parallel"),
        name="prep",
    )(x)


TILE_VMEM_BUDGET = VMEM_LIMIT - 12 * 1024 * 1024


def _row_spec(tm, k, single=False):
    if single:
        return pl.BlockSpec((tm, k), lambda i, j: (i, 0), pipeline_mode=pl.Buffered(1))
    return pl.BlockSpec((tm, k), lambda i, j: (i, 0))


def _scale_spec(tm):
    return pl.BlockSpec((tm, 1), lambda i, j: (i, 0))


def _proj_kernel(a_ref, r_ref, w_ref, o_ref):
    acc = jnp.dot(a_ref[...], w_ref[...], preferred_element_type=F32)
    o_ref[...] = (acc * r_ref[...]).astype(o_ref.dtype)


def proj(a, r, w, layer, out_dtype):
    m, k = a.shape
    n = w.shape[-1]
    osz = jnp.dtype(out_dtype).itemsize
    tm = _pick(m, (1024, 512, 256, 128))
    tn = next(t for t in (1024, 512, 256, 128)
              if n % t == 0 and 2 * tm * k * 2 + 2 * k * t * 2 + 2 * tm * t * osz <= TILE_VMEM_BUDGET)
    return pl.pallas_call(
        _proj_kernel,
        out_shape=jax.ShapeDtypeStruct((m, n), out_dtype),
        grid=(m // tm, n // tn),
        in_specs=[_row_spec(tm, k), _scale_spec(tm),
                  pl.BlockSpec((None, k, tn), lambda i, j: (layer, 0, j))],
        out_specs=pl.BlockSpec((tm, tn), lambda i, j: (i, j)),
        compiler_params=_params("parallel", "arbitrary"),
        name="proj",
    )(a, r, w)


def _swiglu_kernel(a_ref, r_ref, wg_ref, wu_ref, o_ref):
    a = a_ref[...]
    r = r_ref[...]
    g = jnp.dot(a, wg_ref[...], preferred_element_type=F32) * r
    u = jnp.dot(a, wu_ref[...], preferred_element_type=F32) * r
    o_ref[...] = (g * jax.nn.sigmoid(g) * u).astype(o_ref.dtype)


def _swiglu_tail_kernel(a_ref, r_ref, wg_ref, wu_ref, prev_ref, o_ref):
    del prev_ref
    _swiglu_kernel(a_ref, r_ref, wg_ref, wu_ref, o_ref)


SWIGLU_TN = 512


def swiglu_in(a, r, wg, wu, layer):
    m, k = a.shape
    n = wg.shape[-1]
    tm = _pick(m, (1024, 512, 256, 128))
    tn = SWIGLU_TN if n >= SWIGLU_TN else _pick(n, (256, 128))
    n_main = n - n % tn
    out_shape = jax.ShapeDtypeStruct((m, n), BF16)

    def call(kernel_fn, tn_, col0, ncols, extra_in, extra_specs, alias):
        cb = col0 // tn_
        return pl.pallas_call(
            kernel_fn,
            out_shape=out_shape,
            grid=(m // tm, ncols // tn_),
            in_specs=[_row_spec(tm, k), _scale_spec(tm),
                      pl.BlockSpec((None, k, tn_), lambda i, j: (layer, 0, cb + j)),
                      pl.BlockSpec((None, k, tn_), lambda i, j: (layer, 0, cb + j))] + extra_specs,
            out_specs=pl.BlockSpec((tm, tn_), lambda i, j: (i, cb + j)),
            input_output_aliases=alias,
            compiler_params=_params("parallel", "arbitrary"),
            name="swiglu_in",
        )(a, r, wg, wu, *extra_in)

    hid = call(_swiglu_kernel, tn, 0, n_main, [], [], {})
    if n_main < n:
        tail = n - n_main
        assert tail % LANES == 0 and n_main % tail == 0
        hid = call(_swiglu_tail_kernel, tail, n_main, tail, [hid],
                   [pl.BlockSpec(memory_space=pl.ANY)], {4: 0})
    return hid


def _out_kernel(*refs, dtypes, d_model, emit_norm):
    n_parts = len(dtypes)
    a_refs = refs[:n_parts]
    w_refs = refs[n_parts:2 * n_parts]
    res_ref = refs[2 * n_parts]
    if emit_norm:
        o_ref, ob_ref, r_ref, ss_ref = refs[2 * n_parts + 1:2 * n_parts + 5]
        cast_refs = list(refs[2 * n_parts + 5:])
    else:
        o_ref = refs[2 * n_parts + 1]
        cast_refs = list(refs[2 * n_parts + 2:])
    j = pl.program_id(1)

    @pl.when(j == 0)
    def _():
        if emit_norm:
            ss_ref[...] = jnp.zeros_like(ss_ref)
        k = 0
        for a_ref, dt in zip(a_refs, dtypes):
            if dt != BF16:
                cast_refs[k][...] = a_ref[...].astype(BF16)
                k += 1

    acc = res_ref[...]
    k = 0
    for a_ref, w_ref, dt in zip(a_refs, w_refs, dtypes):
        if dt != BF16:
            a = cast_refs[k][...]
            k += 1
        else:
            a = a_ref[...]
        acc = acc + jnp.dot(a, w_ref[...], preferred_element_type=F32)
    o_ref[...] = acc
    if emit_norm:
        ob_ref[...] = acc.astype(BF16)
        ss_ref[...] += jnp.sum(acc * acc, axis=-1, keepdims=True)

        @pl.when(j == pl.num_programs(1) - 1)
        def _():
            r_ref[...] = lax.rsqrt(ss_ref[...] * (1.0 / d_model) + EPS)


def out_proj(parts, w, layer, res, k0=0, emit_norm=True):
    parts = [p if isinstance(p, tuple) else (p, 0, p.shape[1]) for p in parts]
    m = res.shape[0]
    n = w.shape[-1]
    kp = [k for _, _, k in parts]
    dtypes = tuple(a.dtype for a, _, _ in parts)
    out_bytes = 4 + 4 + 2 if emit_norm else 4 + 4

    def fits(tm_, t, nbuf):
        tile_bytes = sum(tm_ * k * a.dtype.itemsize for a, _, k in parts)
        cast_bytes = sum(tm_ * k * 2 for k, dt in zip(kp, dtypes) if dt != BF16)
        return m % tm_ == 0 and n % t == 0 and (nbuf * tile_bytes + cast_bytes + 2 * sum(kp) * t * 2
                                                + 2 * tm_ * t * out_bytes) <= TILE_VMEM_BUDGET

    shapes = [(1024, 1024), (1024, 512), (512, 1024), (1024, 256), (512, 512), (512, 256), (256, 256), (128, 128)]
    shapes = [(a_, b_) for a_, b_ in shapes if m % a_ == 0 and n % b_ == 0] + [(_pick(m, (128, 64, 32, 16, 8)), _pick(n, (128,)))]
    order = [(a_, b_, 2) for a_, b_ in shapes[:3]] + [(a_, b_, nb) for a_, b_ in shapes for nb in (2, 1)]
    tm, tn, nbuf = next(c for c in order if fits(*c))
    in_specs = []
    for _, cb, k in parts:
        if nbuf == 1:
            in_specs.append(pl.BlockSpec((tm, k), lambda i, j, cb=cb: (i, cb), pipeline_mode=pl.Buffered(1)))
        else:
            in_specs.append(pl.BlockSpec((tm, k), lambda i, j, cb=cb: (i, cb)))
    off = k0
    for k in kp:
        assert off % k == 0, "each part must start at a multiple of its own width"
        blk = off // k
        in_specs.append(pl.BlockSpec((None, k, tn), lambda i, j, blk=blk: (layer, blk, j)))
        off += k
    tile = pl.BlockSpec((tm, tn), lambda i, j: (i, j))
    in_specs.append(tile)
    scratch = [pltpu.VMEM((tm, k), BF16) for k, dt in zip(kp, dtypes) if dt != BF16]
    if emit_norm:
        out_shape = (jax.ShapeDtypeStruct((m, n), F32), jax.ShapeDtypeStruct((m, n), BF16),
                     jax.ShapeDtypeStruct((m, 1), F32))
        out_specs = (tile, tile, _scale_spec(tm))
        scratch = [pltpu.VMEM((tm, 1), F32)] + scratch
    else:
        out_shape = jax.ShapeDtypeStruct((m, n), F32)
        out_specs = tile
    return pl.pallas_call(
        functools.partial(_out_kernel, dtypes=dtypes, d_model=n, emit_norm=emit_norm),
        out_shape=out_shape,
        grid=(m // tm, n // tn),
        in_specs=in_specs,
        out_specs=out_specs,
        scratch_shapes=scratch,
        compiler_params=_params("parallel", "arbitrary"),
        name="out_proj",
    )(*[a for a, _, _ in parts], *([w] * len(parts)), res)


def alibi_slopes(n):
    def pow2(m):
        start = 2.0 ** (-(2.0 ** -(math.log2(m) - 3)))
        return [start ** (i + 1) for i in range(m)]
    if math.log2(n).is_integer():
        s = pow2(n)
    else:
        c = 2 ** math.floor(math.log2(n))
        s = pow2(c) + pow2(2 * c)[0::2][: n - c]
    return np.asarray(s, np.float32)


def _attn_kernel(sink_ref, slope_ref, q_ref, kp_ref, kc_ref, kn_ref, vp_ref, vc_ref, vn_ref, o_ref, *, nsub):
    n = pl.program_id(1)
    g = pl.program_id(2)
    nb = pl.num_programs(1) * nsub
    k = jnp.concatenate([kp_ref[...], kc_ref[...], kn_ref[...]], axis=0).astype(BF16)
    v = jnp.concatenate([vp_ref[...], vc_ref[...], vn_ref[...]], axis=0).astype(BF16)
    row = lax.broadcasted_iota(jnp.int32, (BLOCK, 3 * BLOCK), 0)
    col = lax.broadcasted_iota(jnp.int32, (BLOCK, 3 * BLOCK), 1)
    dist = jnp.abs(col - BLOCK - row)
    band = dist <= WINDOW
    distf = dist.astype(F32)
    log2e = math.log2(math.e)
    scale = HEAD_DIM ** -0.5 * log2e
    heads = range(KV_RATIO)
    sinks = [sink_ref[g * KV_RATIO + r] * log2e for r in heads]
    bias = [jnp.where(band, -(slope_ref[g * KV_RATIO + r] * log2e) * distf, NEG_INF) for r in heads]
    for sub in range(nsub):
        blk = n * nsub + sub
        rows = slice(sub * BLOCK, (sub + 1) * BLOCK)
        keys = slice(sub * BLOCK, (sub + 3) * BLOCK)
        kb, vb = k[keys], v[keys]
        edge = None
        if sub == 0:
            edge = col >= jnp.where(blk > 0, 0, BLOCK)
        if sub == nsub - 1:
            hi = col < jnp.where(blk < nb - 1, 3 * BLOCK, 2 * BLOCK)
            edge = hi if edge is None else edge & hi
        s = [lax.dot_general((q_ref[rows, r * HEAD_DIM:(r + 1) * HEAD_DIM] * scale).astype(BF16), kb,
                             (((1,), (1,)), ((), ())), preferred_element_type=F32) + bias[r] for r in heads]
        if edge is not None:
            s = [jnp.where(edge, sr, NEG_INF) for sr in s]
        mx = [jnp.maximum(jnp.max(s[r], axis=-1, keepdims=True), sinks[r]) for r in heads]
        p = [jnp.exp2(s[r] - mx[r]) for r in heads]
        denom = [jnp.sum(p[r], axis=-1, keepdims=True) + jnp.exp2(sinks[r] - mx[r]) for r in heads]
        o = [jnp.dot(p[r].astype(BF16), vb, preferred_element_type=F32) / denom[r] for r in heads]
        for r in heads:
            o_ref[rows, r * HEAD_DIM:(r + 1) * HEAD_DIM] = o[r].astype(o_ref.dtype)


def banded_attention(z, sink, slopes, n_heads):
    b, l, _ = z.shape
    g = n_heads // KV_RATIO
    tq = _pick(l, (512, 256, 128))
    nsub = tq // BLOCK
    nq = l // tq
    nb = l // BLOCK
    qw = KV_RATIO * HEAD_DIM
    kcol = n_heads
    vcol = n_heads + g

    def cur_spec(col0):
        return pl.BlockSpec((None, tq, HEAD_DIM), lambda bi, ni, gi: (bi, ni, col0 + gi))

    def halo_spec(col0, nxt):
        if nxt:
            return pl.BlockSpec((None, BLOCK, HEAD_DIM),
                                lambda bi, ni, gi: (bi, jnp.minimum((ni + 1) * nsub, nb - 1), col0 + gi))
        return pl.BlockSpec((None, BLOCK, HEAD_DIM),
                            lambda bi, ni, gi: (bi, jnp.maximum(ni * nsub - 1, 0), col0 + gi))

    smem = pl.BlockSpec(memory_space=pltpu.SMEM)
    return pl.pallas_call(
        functools.partial(_attn_kernel, nsub=nsub),
        out_shape=jax.ShapeDtypeStruct((b, l, n_heads * HEAD_DIM), BF16),
        grid=(b, nq, g),
        in_specs=[smem, smem,
                  pl.BlockSpec((None, tq, qw), lambda bi, ni, gi: (bi, ni, gi)),
                  halo_spec(kcol, False), cur_spec(kcol), halo_spec(kcol, True),
                  halo_spec(vcol, False), cur_spec(vcol), halo_spec(vcol, True)],
        out_specs=pl.BlockSpec((None, tq, qw), lambda bi, ni, gi: (bi, ni, gi)),
        compiler_params=_params("parallel", "parallel", "arbitrary"),
        name="banded_attention",
    )(sink.astype(F32), slopes, z, z, z, z, z, z, z)


def _split_len(n):
    lg = int(math.log2(n))
    assert 2 ** lg == n
    n2 = 2 ** ((lg + 1) // 2)
    return n // n2, n2


def _cos_sin(n):
    idx = np.arange(n)
    ang = 2.0 * np.pi * ((idx[:, None] * idx[None, :]) % n) / n
    return np.cos(ang), np.sin(ang)


def _twiddle(na, nb, sign):
    n = na * nb
    ang = 2.0 * np.pi * ((np.arange(na)[:, None] * np.arange(nb)[None, :]) % n) / n
    return (jnp.asarray(np.cos(ang)[:, :, None], F32), jnp.asarray(sign * np.sin(ang)[:, :, None], F32))


def _cplx_dft_matrix(n, sign, scale=1.0):
    c, s = _cos_sin(n)
    s = -sign * s
    return np.block([[c, s], [-s, c]]) * scale


def _pack_c(re, im):
    r = lax.bitcast_convert_type(re.astype(BF16).astype(F32), U32)
    i = lax.bitcast_convert_type(im.astype(BF16).astype(F32), U32)
    return r | (i >> 16)


def _unpack_c(p):
    re = lax.bitcast_convert_type(p & jnp.uint32(0xFFFF0000), F32)
    im = lax.bitcast_convert_type(p << 16, F32)
    return jnp.concatenate([re, im], axis=0).astype(BF16)


def _fft1_kernel(x_ref, s_ref, f_ref, twr_ref, twi_ref, a_ref, *, n2):
    scale = s_ref[...]
    f = f_ref[...]
    xt = pltpu.einshape("rsc->src", x_ref[...])
    outs = []
    for s in range(SUBLANES):
        x = (xt[s] * scale).astype(BF16)
        a = jnp.dot(f, x, preferred_element_type=F32)
        ar, ai = a[:n2], a[n2:]
        c, sn = twr_ref[s], twi_ref[s]
        outs.append(_pack_c(ar * c - ai * sn, ar * sn + ai * c))
    a_ref[...] = pltpu.einshape("src->rsc", jnp.stack(outs, axis=0))


def fft_stage1(x, col0, width, n1, n2, rows, scale, kout=None):
    b, _, w = x.shape
    kout = n2 if kout is None else kout
    tc = _pick(math.gcd(math.gcd(w, col0) if col0 else w, width), (512, 256, 128))
    nct = width // tc
    c0 = col0 // tc
    cr, sr = _cos_sin(n2)
    f = jnp.asarray(np.concatenate([cr[:kout], -sr[:kout]], axis=0)[:, :rows], BF16)
    twr, twi = _twiddle(n1, n2, -1.0)
    twr, twi = twr[:, :kout], twi[:, :kout]
    tspec = pl.BlockSpec((SUBLANES, kout, 1), lambda bi, ni, ci: (ni, 0, 0))
    return pl.pallas_call(
        functools.partial(_fft1_kernel, n2=kout),
        out_shape=jax.ShapeDtypeStruct((b, kout, n1, width), U32),
        grid=(b, n1 // SUBLANES, nct),
        in_specs=[pl.BlockSpec((None, rows, SUBLANES, tc), lambda bi, ni, ci: (bi, 0, ni, c0 + ci)),
                  pl.BlockSpec((1, tc), lambda bi, ni, ci: (0, ci)),
                  pl.BlockSpec((2 * kout, rows), lambda bi, ni, ci: (0, 0)),
                  tspec, tspec],
        out_specs=pl.BlockSpec((None, kout, SUBLANES, tc), lambda bi, ni, ci: (bi, 0, ni, ci)),
        compiler_params=_params("parallel", "parallel", "arbitrary"),
        name="fft_stage1",
    )(x.reshape(b, rows, n1, w), scale, f, twr, twi)


def _fourier2_kernel(a_ref, m_ref, cs_ref, o_ref, *, n1, groups):
    outs = []
    for s in range(SUBLANES):
        gm = jnp.dot(m_ref[...], _unpack_c(a_ref[s]), preferred_element_type=F32)
        gr, gi = gm[:n1].astype(BF16), gm[n1:].astype(BF16)
        cols = []
        for q in range(groups):
            sl = slice(q * HEAD_DIM, (q + 1) * HEAD_DIM)
            lhs = jnp.concatenate([gr[:, sl], gi[:, sl]], axis=1)
            cols.append(jnp.dot(lhs, cs_ref[...], preferred_element_type=F32))
        outs.append(jnp.concatenate(cols, axis=1))
    o_ref[...] = pltpu.einshape("src->rsc", jnp.stack(outs, axis=0))


def fourier_mix(z, col0, width):
    b, l, _ = z.shape
    n1, n2 = _split_len(l)
    ones = jnp.ones((1, width), F32)
    a = fft_stage1(z, col0, width, n1, n2, n2, ones)
    m = jnp.asarray(_cplx_dft_matrix(n1, -1.0), BF16)
    cc, sc = _cos_sin(HEAD_DIM)
    cs = jnp.asarray(np.concatenate([cc, sc], axis=0) / math.sqrt(l * HEAD_DIM), BF16)
    groups = width // HEAD_DIM
    out = pl.pallas_call(
        functools.partial(_fourier2_kernel, n1=n1, groups=groups),
        out_shape=jax.ShapeDtypeStruct((b, n1, n2, width), F32),
        grid=(b, n2 // SUBLANES),
        in_specs=[pl.BlockSpec((None, SUBLANES, n1, width), lambda bi, ki: (bi, ki, 0, 0)),
                  pl.BlockSpec((2 * n1, 2 * n1), lambda bi, ki: (0, 0)),
                  pl.BlockSpec((2 * HEAD_DIM, HEAD_DIM), lambda bi, ki: (0, 0))],
        out_specs=pl.BlockSpec((None, n1, SUBLANES, width), lambda bi, ki: (bi, 0, ki, 0)),
        compiler_params=_params("parallel", "arbitrary"),
        name="fourier_stage2",
    )(a, m, cs)
    return out.reshape(b, l, width)


def _conformer_kernel(ap_ref, ac_ref, an_ref, gp_ref, gc_ref, gn_ref, w_ref, b_ref, lg_ref, lb_ref,
                      o_ref, buf_ref, sh_ref, conv_ref, *, tl, width):
    t = pl.program_id(1)
    nt = pl.num_programs(1)
    h = CONV_HALO
    prev = ap_ref[...] * jax.nn.sigmoid(gp_ref[...])
    nxt = an_ref[...] * jax.nn.sigmoid(gn_ref[...])
    buf_ref[0:h, :] = jnp.where(t > 0, prev, 0.0)
    buf_ref[h:h + tl, :] = ac_ref[...] * jax.nn.sigmoid(gc_ref[...])
    buf_ref[h + tl:h + tl + h, :] = jnp.where(t < nt - 1, nxt, 0.0)
    span = tl + 2 * h - SUBLANES
    for ph in range(SUBLANES):
        sh_ref[ph, 0:span, :] = buf_ref[ph:ph + span, :]
    base = h - CONV_KERNEL // 2
    for c in range(width // LANES):
        sl = slice(c * LANES, (c + 1) * LANES)
        acc = jnp.broadcast_to(b_ref[:, sl], (SUBLANES, LANES))[None]
        for j in range(CONV_KERNEL):
            a8, ph = divmod(base + j, SUBLANES)
            wj = jnp.broadcast_to(w_ref[j:j + 1, sl], (SUBLANES, LANES))[None]
            xs = sh_ref[ph, a8 * SUBLANES:a8 * SUBLANES + tl, sl].reshape(tl // SUBLANES, SUBLANES, LANES)
            acc = acc + wj * xs
        conv_ref[:, sl] = acc.reshape(tl, LANES)
    y = conv_ref[...]
    mu = jnp.mean(y, axis=-1, keepdims=True)
    yc = y - mu
    var = jnp.mean(yc * yc, axis=-1, keepdims=True)
    yn = yc * lax.rsqrt(var + EPS) * lg_ref[...] + lb_ref[...]
    o_ref[...] = (yn * jax.nn.sigmoid(yn)).astype(o_ref.dtype)


def conformer_conv(z, width, dw_w, dw_b, ln_g, ln_b):
    b, l, _ = z.shape
    tl = _pick(l, (128, 64, 32, 16))
    h = CONV_HALO
    r = tl // h
    nh = l // h

    def cur(col):
        return pl.BlockSpec((None, tl, width), lambda bi, ti: (bi, ti, col))

    def halo(col, nxt):
        if nxt:
            return pl.BlockSpec((None, h, width), lambda bi, ti: (bi, jnp.minimum((ti + 1) * r, nh - 1), col))
        return pl.BlockSpec((None, h, width), lambda bi, ti: (bi, jnp.maximum(ti * r - 1, 0), col))

    vec = pl.BlockSpec((1, width), lambda bi, ti: (0, 0))
    return pl.pallas_call(
        functools.partial(_conformer_kernel, tl=tl, width=width),
        out_shape=jax.ShapeDtypeStruct((b, l, width), BF16),
        grid=(b, l // tl),
        in_specs=[halo(0, False), cur(0), halo(0, True), halo(1, False), cur(1), halo(1, True),
                  pl.BlockSpec((CONV_KERNEL, width), lambda bi, ti: (0, 0)), vec, vec, vec],
        out_specs=pl.BlockSpec((None, tl, width), lambda bi, ti: (bi, ti, 0)),
        scratch_shapes=[pltpu.VMEM((tl + 2 * h, width), F32),
                        pltpu.VMEM((SUBLANES, tl + 2 * h, width), F32),
                        pltpu.VMEM((tl, width), F32)],
        compiler_params=_params("parallel", "arbitrary"),
        name="conformer_conv",
    )(z, z, z, z, z, z, dw_w, dw_b.reshape(1, width), ln_g.reshape(1, width), ln_b.reshape(1, width))


def _short_conv(prev_ref, cur_ref, next_ref, w_ref, b_ref, first, last, tl):
    x = cur_ref[...]
    rows = lax.broadcasted_iota(jnp.int32, x.shape, 0)
    pr = jnp.where(first, 0.0, prev_ref[SUBLANES - 1:SUBLANES, :])
    nx = jnp.where(last, 0.0, next_ref[0:1, :])
    xm = jnp.where(rows == 0, pr, pltpu.roll(x, 1, axis=0))
    xp = jnp.where(rows == tl - 1, nx, pltpu.roll(x, tl - 1, axis=0))
    return w_ref[0:1, :] * xm + w_ref[1:2, :] * x + w_ref[2:3, :] * xp + b_ref[...]


def _hyena_gate_kernel(*refs, tl):
    x0 = refs[0:3]
    x1 = refs[3:6]
    xv = refs[6:9]
    w0, w1, wv, b0, b1, bv, x0_out, v_out = refs[9:]
    t = pl.program_id(1)
    first = t == 0
    last = t == pl.num_programs(1) - 1
    x0_out[...] = _short_conv(*x0, w0, b0, first, last, tl)
    v_out[...] = _short_conv(*xv, wv, bv, first, last, tl) * _short_conv(*x1, w1, b1, first, last, tl)


def hyena_gate(z, col0, width, sw, sb):
    b, l, _ = z.shape
    tl = _pick(l, (512, 256, 128, 64, 32, 16, 8))
    tc = _pick(math.gcd(col0, width), (512, 256, 128))
    r = tl // SUBLANES
    nh = l // SUBLANES
    nct = width // tc

    def trio(part):
        c0 = (col0 + part * width) // tc
        return [pl.BlockSpec((None, SUBLANES, tc), lambda bi, ti, ci: (bi, jnp.maximum(ti * r - 1, 0), c0 + ci)),
                pl.BlockSpec((None, tl, tc), lambda bi, ti, ci: (bi, ti, c0 + ci)),
                pl.BlockSpec((None, SUBLANES, tc), lambda bi, ti, ci: (bi, jnp.minimum((ti + 1) * r, nh - 1), c0 + ci))]

    def wspec(part, rows):
        return pl.BlockSpec((rows, tc), lambda bi, ti, ci: (0, part * nct + ci))

    out = jax.ShapeDtypeStruct((b, l, width), F32)
    ospec = pl.BlockSpec((None, tl, tc), lambda bi, ti, ci: (bi, ti, ci))
    sb2 = sb.reshape(1, 3 * width)
    return pl.pallas_call(
        functools.partial(_hyena_gate_kernel, tl=tl),
        out_shape=(out, out),
        grid=(b, l // tl, nct),
        in_specs=trio(0) + trio(1) + trio(2) + [wspec(0, HYENA_SHORT), wspec(1, HYENA_SHORT), wspec(2, HYENA_SHORT),
                                                 wspec(0, 1), wspec(1, 1), wspec(2, 1)],
        out_specs=(ospec, ospec),
        compiler_params=_params("parallel", "parallel", "arbitrary"),
        name="hyena_gate",
    )(*([z] * 9), sw, sw, sw, sb2, sb2, sb2)


def _hdot(a, b):
    return jnp.dot(a, b, preferred_element_type=F32, precision=lax.Precision.HIGHEST)


def _filter_kernel(bands_ref, w1t_ref, w1c_ref, w1s_ref, b1_ref, w2_ref, b2_ref, w3_ref, b3_ref, fr_ref,
                   w4_ref, dec_ref, k_ref, norm_ref, *, l, tl):
    i = pl.program_id(0)
    m = i * tl + lax.broadcasted_iota(jnp.int32, (tl, 1), 0)
    j = jnp.where(m < l, m, 2 * l - m).astype(F32)
    t = j / (l - 1.0)
    ang = (2.0 * math.pi / l) * bands_ref[...] * j
    fr = fr_ref[...]
    pre = t * w1t_ref[...] + _hdot(jnp.cos(ang), w1c_ref[...]) - _hdot(jnp.sin(ang), w1s_ref[...])
    hcur = jnp.sin(fr * (pre + b1_ref[...]))
    hcur = jnp.sin(fr * (_hdot(hcur, w2_ref[...]) + b2_ref[...]))
    hcur = jnp.sin(fr * (_hdot(hcur, w3_ref[...]) + b3_ref[...]))
    out = jnp.dot(hcur.astype(BF16), w4_ref[...].astype(BF16), preferred_element_type=F32) * jnp.exp(-t * dec_ref[...])
    out = jnp.where(m == l, 0.0, out)
    k_ref[...] = out

    @pl.when(i == 0)
    def _():
        norm_ref[...] = jnp.zeros_like(norm_ref)

    norm_ref[...] += jnp.sum(jnp.abs(out), axis=0, keepdims=True)


def hyena_filter(l, width, w1, b1, w2, b2, w3, b3, freq, w4):
    hid = w1.shape[1]
    tl = _pick(l, (512, 256, 128, 64, 32, 16, 8))
    nl = l // tl
    bands = jnp.asarray(np.linspace(1e-4, HYENA_BANDS - 1, HYENA_BANDS, dtype=np.float32)[None, :])
    dec = jnp.asarray(np.abs(np.linspace(HYENA_MIN_DECAY, HYENA_MAX_DECAY, width, dtype=np.float32))[None, :])
    full = lambda shape: pl.BlockSpec(shape, lambda i: tuple(0 for _ in shape))
    w4v = w4.reshape(hid, 2, width).transpose(1, 0, 2)
    return pl.pallas_call(
        functools.partial(_filter_kernel, l=l, tl=tl),
        out_shape=(jax.ShapeDtypeStruct((2 * l, width), F32), jax.ShapeDtypeStruct((1, width), F32)),
        grid=(2 * nl,),
        in_specs=[full((1, HYENA_BANDS)), full((1, hid)), full((HYENA_BANDS, hid)), full((HYENA_BANDS, hid)),
                  full((1, hid)), full((hid, hid)), full((1, hid)), full((hid, hid)), full((1, hid)), full((1, hid)),
                  pl.BlockSpec((None, hid, width), lambda i: (i // nl, 0, 0)),
                  full((1, width))],
        out_specs=(pl.BlockSpec((tl, width), lambda i: (i, 0)), full((1, width))),
        compiler_params=_params("arbitrary"),
        name="hyena_filter",
    )(bands, w1[0:1], w1[1:1 + HYENA_BANDS], w1[1 + HYENA_BANDS:], b1.reshape(1, hid), w2, b2.reshape(1, hid),
      w3, b3.reshape(1, hid), freq.reshape(1, hid), w4v, dec)


def _spectrum2_kernel(a_ref, m_ref, k_ref, *, n1, kb):
    for s in range(kb):
        k = jnp.dot(m_ref[...], _unpack_c(a_ref[s]), preferred_element_type=F32)
        k_ref[s] = _pack_c(k[:n1], k[n1:])


MID_BLOCK = 2


def _half_residues(n2):
    return n2 // 2 + SUBLANES


def filter_spectrum(kern, norm, n1, n2):
    nlen, width = kern.shape
    n2h = _half_residues(n2)
    a = fft_stage1(kern.reshape(1, nlen, width), 0, width, n1, n2, n2, 1.0 / norm, kout=n2h)
    m = jnp.asarray(_cplx_dft_matrix(n1, -1.0), BF16)
    kb = MID_BLOCK
    spec = pl.BlockSpec((kb, n1, width), lambda ki: (ki, 0, 0))
    return pl.pallas_call(
        functools.partial(_spectrum2_kernel, n1=n1, kb=kb),
        out_shape=jax.ShapeDtypeStruct((n2h, n1, width), U32),
        grid=(n2h // kb,),
        in_specs=[spec, pl.BlockSpec((2 * n1, 2 * n1), lambda ki: (0, 0))],
        out_specs=spec,
        compiler_params=_params("arbitrary"),
        name="filter_spectrum",
    )(a.reshape(n2h, n1, width), m)


def _hyena_mid_kernel(a_ref, k_ref, mf_ref, mi_ref, twr_ref, twi_ref, b_ref, *, n1, kb):
    for s in range(kb):
        v = jnp.dot(mf_ref[...], _unpack_c(a_ref[s]), preferred_element_type=F32)
        vr, vi = v[:n1], v[n1:]
        kp = k_ref[s]
        kr = lax.bitcast_convert_type(kp & jnp.uint32(0xFFFF0000), F32)
        ki = lax.bitcast_convert_type(kp << 16, F32)
        y = jnp.concatenate([vr * kr - vi * ki, vr * ki + vi * kr], axis=0).astype(BF16)
        bm = jnp.dot(mi_ref[...], y, preferred_element_type=F32)
        br, bi = bm[:n1], bm[n1:]
        c, sn = twr_ref[s], twi_ref[s]
        b_ref[s] = _pack_c(br * c - bi * sn, br * sn + bi * c)


def _hyena_out_kernel(b_ref, f_ref, v_ref, x0_ref, bias_ref, o_ref):
    bias = bias_ref[...]
    f = f_ref[...]
    bt = pltpu.einshape("rsc->src", b_ref[...])
    ys = [jnp.dot(f, _unpack_c(bt[s]), preferred_element_type=F32) for s in range(SUBLANES)]
    y = pltpu.einshape("src->rsc", jnp.stack(ys, axis=0))
    o_ref[...] = x0_ref[...] * (y + v_ref[...] * bias)


def hyena_long_conv(v, x0, kspec, bias, n1, n2):
    b, l, width = v.shape
    nlen = n1 * n2
    half = n2 // 2
    n2h = _half_residues(n2)
    ones = jnp.ones((1, width), F32)
    a = fft_stage1(v, 0, width, n1, n2, half, ones, kout=n2h)
    mf = jnp.asarray(_cplx_dft_matrix(n1, -1.0), BF16)
    mi = jnp.asarray(_cplx_dft_matrix(n1, 1.0), BF16)
    twr, twi = _twiddle(n2, n1, 1.0)
    twr, twi = twr[:n2h], twi[:n2h]
    kb = MID_BLOCK
    aspec = pl.BlockSpec((None, kb, n1, width), lambda ki_, bi: (bi, ki_, 0, 0))
    mspec = pl.BlockSpec((2 * n1, 2 * n1), lambda ki_, bi: (0, 0))
    tspec = pl.BlockSpec((kb, n1, 1), lambda ki_, bi: (ki_, 0, 0))
    bm = pl.pallas_call(
        functools.partial(_hyena_mid_kernel, n1=n1, kb=kb),
        out_shape=jax.ShapeDtypeStruct((b, n2h, n1, width), U32),
        grid=(n2h // kb, b),
        in_specs=[aspec, pl.BlockSpec((kb, n1, width), lambda ki_, bi: (ki_, 0, 0)), mspec, mspec, tspec, tspec],
        out_specs=aspec,
        compiler_params=_params("parallel", "arbitrary"),
        name="hyena_mid",
    )(a, kspec, mf, mi, twr, twi)

    tc = _pick(width, (512, 256, 128))
    nct = width // tc
    cr, sr = _cos_sin(n2)
    coef = np.where(np.arange(n2h) > half, 0.0, np.where((np.arange(n2h) == 0) | (np.arange(n2h) == half), 1.0, 2.0))
    finv = jnp.asarray(np.concatenate([cr[:half, :n2h] * coef, -sr[:half, :n2h] * coef], axis=1) / nlen, BF16)
    sv = (b, half, n1, width)
    sspec = pl.BlockSpec((None, half, SUBLANES, tc), lambda bi, ti, ci: (bi, 0, ti, ci))
    out = pl.pallas_call(
        _hyena_out_kernel,
        out_shape=jax.ShapeDtypeStruct(sv, F32),
        grid=(b, n1 // SUBLANES, nct),
        in_specs=[pl.BlockSpec((None, n2h, SUBLANES, tc), lambda bi, ti, ci: (bi, 0, ti, ci)),
                  pl.BlockSpec((half, 2 * n2h), lambda bi, ti, ci: (0, 0)),
                  sspec, sspec, pl.BlockSpec((1, tc), lambda bi, ti, ci: (0, ci))],
        out_specs=sspec,
        compiler_params=_params("parallel", "parallel", "arbitrary"),
        name="hyena_out",
    )(bm, finv, v.reshape(sv), x0.reshape(sv), bias.reshape(1, width))
    return out.reshape(b, l, width)


def _trunk(x, p, wb):
    b, l, d = x.shape
    m = b * l
    depth = p['mix_norm'].shape[0]
    fourier_w = d // 4
    attn_w = d - fourier_w
    n_heads = attn_w // HEAD_DIM
    kv_w = (n_heads // KV_RATIO) * HEAD_DIM
    conv_w = d // 2
    hy_w = d - conv_w
    slopes = jnp.asarray(alibi_slopes(n_heads))
    hn1, hn2 = _split_len(2 * l)

    xf = x.reshape(m, d)
    xb, r = prep(xf)
    for layer in range(depth):
        i = layer // 2
        if layer % 2 == 0:
            z = proj(xb, r, wb['ab_w_in'], i, F32).reshape(b, l, -1)
            a = banded_attention(z, p['attn_sink'][i], slopes, n_heads)
            f = fourier_mix(z, attn_w + 2 * kv_w, fourier_w)
            parts = [a.reshape(m, attn_w), f.reshape(m, fourier_w)]
            xf, xb, r = out_proj(parts, wb['ab_w_out'], i, xf)
        else:
            z = proj(xb, r, wb['cd_w_in'], i, F32).reshape(b, l, -1)
            c = conformer_conv(z, conv_w, p['conv_dw_w'][i], p['conv_dw_b'][i], p['conv_ln_g'][i], p['conv_ln_b'][i])
            x0, v = hyena_gate(z, 2 * conv_w, hy_w, p['hy_short_w'][i], p['hy_short_b'][i])
            kern, norm = hyena_filter(l, hy_w, p['hy_filt_w1'][i], p['hy_filt_b1'][i], p['hy_filt_w2'][i],
                                      p['hy_filt_b2'][i], p['hy_filt_w3'][i], p['hy_filt_b3'][i],
                                      p['hy_filt_freq'][i], p['hy_filt_w4'][i])
            kspec = filter_spectrum(kern, norm, hn1, hn2)
            dd = hyena_long_conv(v, x0, kspec, p['hy_bias'][i], hn1, hn2)
            parts = [c.reshape(m, conv_w), dd.reshape(m, hy_w)]
            xf, xb, r = out_proj(parts, wb['cd_w_out'], i, xf)
        hid = swiglu_in(xb, r, wb['w_gate'], wb['w_up'], layer)
        kh = hid.shape[1] // 2
        assert kh % LANES == 0
        xf = out_proj([(hid, 0, kh)], wb['w_down'], layer, xf, emit_norm=False)
        xf, xb, r = out_proj([(hid, 1, kh)], wb['w_down'], layer, xf, k0=kh)
    return rms_norm(xf, p['final_norm'], F32).reshape(b, l, d)


def kernel(x_prompt, x_sample, mix_norm, ffn_norm, final_norm, w_gate, w_up, w_down, ab_w_in, ab_w_out, attn_sink, cd_w_in, cd_w_out, conv_dw_w, conv_dw_b, conv_ln_g, conv_ln_b, hy_short_w, hy_short_b, hy_filt_w1, hy_filt_b1, hy_filt_w2, hy_filt_b2, hy_filt_w3, hy_filt_b3, hy_filt_freq, hy_filt_w4, hy_bias):
    p = dict(mix_norm=mix_norm, ffn_norm=ffn_norm, final_norm=final_norm, attn_sink=attn_sink,
             conv_dw_w=conv_dw_w, conv_dw_b=conv_dw_b, conv_ln_g=conv_ln_g, conv_ln_b=conv_ln_b,
             hy_short_w=hy_short_w, hy_short_b=hy_short_b, hy_filt_w1=hy_filt_w1, hy_filt_b1=hy_filt_b1,
             hy_filt_w2=hy_filt_w2, hy_filt_b2=hy_filt_b2, hy_filt_w3=hy_filt_w3, hy_filt_b3=hy_filt_b3,
             hy_filt_freq=hy_filt_freq, hy_filt_w4=hy_filt_w4, hy_bias=hy_bias)
    g_ab, g_cd, g_ffn = mix_norm[0::2, :, None], mix_norm[1::2, :, None], ffn_norm[:, :, None]
    wb = dict(w_gate=(g_ffn * w_gate).astype(BF16), w_up=(g_ffn * w_up).astype(BF16), w_down=w_down.astype(BF16),
              ab_w_in=(g_ab * ab_w_in).astype(BF16), ab_w_out=ab_w_out.astype(BF16),
              cd_w_in=(g_cd * cd_w_in).astype(BF16), cd_w_out=cd_w_out.astype(BF16))
    return (_trunk(x_prompt, p, wb), _trunk(x_sample, p, wb))
```

```python
import functools
import math

import numpy as np
import jax
import jax.numpy as jnp
from jax import lax
from jax.experimental import pallas as pl
from jax.experimental.pallas import tpu as pltpu

F32 = jnp.float32
BF16 = jnp.bfloat16
U32 = jnp.uint32

HEAD_DIM = 128
KV_RATIO = 4
WINDOW = 128
BLOCK = 128
CONV_KERNEL = 31
CONV_HALO = 16
HYENA_SHORT = 3
HYENA_BANDS = 16
HYENA_MIN_DECAY = math.log(1e-2) / 1.5
HYENA_MAX_DECAY = math.log(1e-2) / 0.3
EPS = 1e-6
NEG_INF = -1e30

V7X_VMEM_BYTES = 64 * 1024 * 1024
VMEM_LIMIT = V7X_VMEM_BYTES - 8 * 1024 * 1024
LANES = 128
SUBLANES = 8


def _params(*sem):
    return pltpu.CompilerParams(dimension_semantics=sem, vmem_limit_bytes=VMEM_LIMIT)


def _pick(n, candidates):
    for c in candidates:
        if c <= n and n % c == 0:
            return c
    return n


def _rms_kernel(x_ref, g_ref, o_ref):
    x = x_ref[...]
    ms = jnp.mean(x * x, axis=-1, keepdims=True)
    o_ref[...] = (x * lax.rsqrt(ms + EPS) * g_ref[...]).astype(o_ref.dtype)


def rms_norm(x, g, out_dtype):
    m, d = x.shape
    tr = _pick(m, (256, 128, 64, 32, 16, 8))
    return pl.pallas_call(
        _rms_kernel,
        out_shape=jax.ShapeDtypeStruct((m, d), out_dtype),
        grid=(m // tr,),
        in_specs=[pl.BlockSpec((tr, d), lambda i: (i, 0)),
                  pl.BlockSpec((1, d), lambda i: (0, 0))],
        out_specs=pl.BlockSpec((tr, d), lambda i: (i, 0)),
        compiler_params=_params("parallel"),
        name="rms_norm",
    )(x, g.reshape(1, d))


def _prep_kernel(x_ref, xb_ref, r_ref):
    x = x_ref[...]
    xb_ref[...] = x.astype(BF16)
    r_ref[...] = lax.rsqrt(jnp.mean(x * x, axis=-1, keepdims=True) + EPS)


def prep(x):
    m, d = x.shape
    tr = _pick(m, (256, 128, 64, 32, 16, 8))
    return pl.pallas_call(
        _prep_kernel,
        out_shape=(jax.ShapeDtypeStruct((m, d), BF16), jax.ShapeDtypeStruct((m, 1), F32)),
        grid=(m // tr,),
        in_specs=[pl.BlockSpec((tr, d), lambda i: (i, 0))],
        out_specs=(pl.BlockSpec((tr, d), lambda i: (i, 0)), pl.BlockSpec((tr, 1), lambda i: (i, 0))),
        compiler_params=_params("parallel"),
        name="prep",
    )(x)


TILE_VMEM_BUDGET = VMEM_LIMIT - 12 * 1024 * 1024


def _row_spec(tm, k, single=False):
    if single:
        return pl.BlockSpec((tm, k), lambda i, j: (i, 0), pipeline_mode=pl.Buffered(1))
    return pl.BlockSpec((tm, k), lambda i, j: (i, 0))


def _scale_spec(tm):
    return pl.BlockSpec((tm, 1), lambda i, j: (i, 0))


def _proj_kernel(a_ref, r_ref, w_ref, o_ref):
    acc = jnp.dot(a_ref[...], w_ref[...], preferred_element_type=F32)
    o_ref[...] = (acc * r_ref[...]).astype(o_ref.dtype)


def proj(a, r, w, layer, out_dtype):
    m, k = a.shape
    n = w.shape[-1]
    osz = jnp.dtype(out_dtype).itemsize
    tm = _pick(m, (1024, 512, 256, 128))
    tn = next(t for t in (1024, 512, 256, 128)
              if n % t == 0 and 2 * tm * k * 2 + 2 * k * t * 2 + 2 * tm * t * osz <= TILE_VMEM_BUDGET)
    return pl.pallas_call(
        _proj_kernel,
        out_shape=jax.ShapeDtypeStruct((m, n), out_dtype),
        grid=(m // tm, n // tn),
        in_specs=[_row_spec(tm, k), _scale_spec(tm),
                  pl.BlockSpec((None, k, tn), lambda i, j: (layer, 0, j))],
        out_specs=pl.BlockSpec((tm, tn), lambda i, j: (i, j)),
        compiler_params=_params("parallel", "arbitrary"),
        name="proj",
    )(a, r, w)


def _swiglu_kernel(a_ref, r_ref, wg_ref, wu_ref, o_ref):
    a = a_ref[...]
    r = r_ref[...]
    g = jnp.dot(a, wg_ref[...], preferred_element_type=F32) * r
    u = jnp.dot(a, wu_ref[...], preferred_element_type=F32) * r
    o_ref[...] = (g * jax.nn.sigmoid(g) * u).astype(o_ref.dtype)


def _swiglu_tail_kernel(a_ref, r_ref, wg_ref, wu_ref, prev_ref, o_ref):
    del prev_ref
    _swiglu_kernel(a_ref, r_ref, wg_ref, wu_ref, o_ref)


SWIGLU_TN = 512


def swiglu_in(a, r, wg, wu, layer):
    m, k = a.shape
    n = wg.shape[-1]
    tm = _pick(m, (1024, 512, 256, 128))
    tn = SWIGLU_TN if n >= SWIGLU_TN else _pick(n, (256, 128))
    n_main = n - n % tn
    out_shape = jax.ShapeDtypeStruct((m, n), BF16)

    def call(kernel_fn, tn_, col0, ncols, extra_in, extra_specs, alias):
        cb = col0 // tn_
        return pl.pallas_call(
            kernel_fn,
            out_shape=out_shape,
            grid=(m // tm, ncols // tn_),
            in_specs=[_row_spec(tm, k), _scale_spec(tm),
                      pl.BlockSpec((None, k, tn_), lambda i, j: (layer, 0, cb + j)),
                      pl.BlockSpec((None, k, tn_), lambda i, j: (layer, 0, cb + j))] + extra_specs,
            out_specs=pl.BlockSpec((tm, tn_), lambda i, j: (i, cb + j)),
            input_output_aliases=alias,
            compiler_params=_params("parallel", "arbitrary"),
            name="swiglu_in",
        )(a, r, wg, wu, *extra_in)

    hid = call(_swiglu_kernel, tn, 0, n_main, [], [], {})
    if n_main < n:
        tail = n - n_main
        assert tail % LANES == 0 and n_main % tail == 0
        hid = call(_swiglu_tail_kernel, tail, n_main, tail, [hid],
                   [pl.BlockSpec(memory_space=pl.ANY)], {4: 0})
    return hid


def _out_kernel(*refs, dtypes, d_model, emit_norm):
    n_parts = len(dtypes)
    a_refs = refs[:n_parts]
    w_refs = refs[n_parts:2 * n_parts]
    res_ref = refs[2 * n_parts]
    if emit_norm:
        o_ref, ob_ref, r_ref, ss_ref = refs[2 * n_parts + 1:2 * n_parts + 5]
        cast_refs = list(refs[2 * n_parts + 5:])
    else:
        o_ref = refs[2 * n_parts + 1]
        cast_refs = list(refs[2 * n_parts + 2:])
    j = pl.program_id(1)

    @pl.when(j == 0)
    def _():
        if emit_norm:
            ss_ref[...] = jnp.zeros_like(ss_ref)
        k = 0
        for a_ref, dt in zip(a_refs, dtypes):
            if dt != BF16:
                cast_refs[k][...] = a_ref[...].astype(BF16)
                k += 1

    acc = res_ref[...]
    k = 0
    for a_ref, w_ref, dt in zip(a_refs, w_refs, dtypes):
        if dt != BF16:
            a = cast_refs[k][...]
            k += 1
        else:
            a = a_ref[...]
        acc = acc + jnp.dot(a, w_ref[...], preferred_element_type=F32)
    o_ref[...] = acc
    if emit_norm:
        ob_ref[...] = acc.astype(BF16)
        ss_ref[...] += jnp.sum(acc * acc, axis=-1, keepdims=True)

        @pl.when(j == pl.num_programs(1) - 1)
        def _():
            r_ref[...] = lax.rsqrt(ss_ref[...] * (1.0 / d_model) + EPS)


def out_proj(parts, w, layer, res, k0=0, emit_norm=True):
    parts = [p if isinstance(p, tuple) else (p, 0, p.shape[1]) for p in parts]
    m = res.shape[0]
    n = w.shape[-1]
    kp = [k for _, _, k in parts]
    dtypes = tuple(a.dtype for a, _, _ in parts)
    out_bytes = 4 + 4 + 2 if emit_norm else 4 + 4

    def fits(tm_, t, nbuf):
        tile_bytes = sum(tm_ * k * a.dtype.itemsize for a, _, k in parts)
        cast_bytes = sum(tm_ * k * 2 for k, dt in zip(kp, dtypes) if dt != BF16)
        return m % tm_ == 0 and n % t == 0 and (nbuf * tile_bytes + cast_bytes + 2 * sum(kp) * t * 2
                                                + 2 * tm_ * t * out_bytes) <= TILE_VMEM_BUDGET

    shapes = [(1024, 1024), (1024, 512), (512, 1024), (1024, 256), (512, 512), (512, 256), (256, 256), (128, 128)]
    shapes = [(a_, b_) for a_, b_ in shapes if m % a_ == 0 and n % b_ == 0] + [(_pick(m, (128, 64, 32, 16, 8)), _pick(n, (128,)))]
    order = [(a_, b_, 2) for a_, b_ in shapes[:3]] + [(a_, b_, nb) for a_, b_ in shapes for nb in (2, 1)]
    tm, tn, nbuf = next(c for c in order if fits(*c))
    in_specs = []
    for _, cb, k in parts:
        if nbuf == 1:
            in_specs.append(pl.BlockSpec((tm, k), lambda i, j, cb=cb: (i, cb), pipeline_mode=pl.Buffered(1)))
        else:
            in_specs.append(pl.BlockSpec((tm, k), lambda i, j, cb=cb: (i, cb)))
    off = k0
    for k in kp:
        assert off % k == 0, "each part must start at a multiple of its own width"
        blk = off // k
        in_specs.append(pl.BlockSpec((None, k, tn), lambda i, j, blk=blk: (layer, blk, j)))
        off += k
    tile = pl.BlockSpec((tm, tn), lambda i, j: (i, j))
    in_specs.append(tile)
    scratch = [pltpu.VMEM((tm, k), BF16) for k, dt in zip(kp, dtypes) if dt != BF16]
    if emit_norm:
        out_shape = (jax.ShapeDtypeStruct((m, n), F32), jax.ShapeDtypeStruct((m, n), BF16),
                     jax.ShapeDtypeStruct((m, 1), F32))
        out_specs = (tile, tile, _scale_spec(tm))
        scratch = [pltpu.VMEM((tm, 1), F32)] + scratch
    else:
        out_shape = jax.ShapeDtypeStruct((m, n), F32)
        out_specs = tile
    return pl.pallas_call(
        functools.partial(_out_kernel, dtypes=dtypes, d_model=n, emit_norm=emit_norm),
        out_shape=out_shape,
        grid=(m // tm, n // tn),
        in_specs=in_specs,
        out_specs=out_specs,
        scratch_shapes=scratch,
        compiler_params=_params("parallel", "arbitrary"),
        name="out_proj",
    )(*[a for a, _, _ in parts], *([w] * len(parts)), res)


def alibi_slopes(n):
    def pow2(m):
        start = 2.0 ** (-(2.0 ** -(math.log2(m) - 3)))
        return [start ** (i + 1) for i in range(m)]
    if math.log2(n).is_integer():
        s = pow2(n)
    else:
        c = 2 ** math.floor(math.log2(n))
        s = pow2(c) + pow2(2 * c)[0::2][: n - c]
    return np.asarray(s, np.float32)


def _attn_kernel(sink_ref, slope_ref, q_ref, kp_ref, kc_ref, kn_ref, vp_ref, vc_ref, vn_ref, o_ref, *, nsub):
    n = pl.program_id(1)
    g = pl.program_id(2)
    nb = pl.num_programs(1) * nsub
    k = jnp.concatenate([kp_ref[...], kc_ref[...], kn_ref[...]], axis=0).astype(BF16)
    v = jnp.concatenate([vp_ref[...], vc_ref[...], vn_ref[...]], axis=0).astype(BF16)
    row = lax.broadcasted_iota(jnp.int32, (BLOCK, 3 * BLOCK), 0)
    col = lax.broadcasted_iota(jnp.int32, (BLOCK, 3 * BLOCK), 1)
    dist = jnp.abs(col - BLOCK - row)
    band = dist <= WINDOW
    distf = dist.astype(F32)
    log2e = math.log2(math.e)
    scale = HEAD_DIM ** -0.5 * log2e
    heads = range(KV_RATIO)
    sinks = [sink_ref[g * KV_RATIO + r] * log2e for r in heads]
    bias = [jnp.where(band, -(slope_ref[g * KV_RATIO + r] * log2e) * distf, NEG_INF) for r in heads]
    for sub in range(nsub):
        blk = n * nsub + sub
        rows = slice(sub * BLOCK, (sub + 1) * BLOCK)
        keys = slice(sub * BLOCK, (sub + 3) * BLOCK)
        kb, vb = k[keys], v[keys]
        edge = None
        if sub == 0:
            edge = col >= jnp.where(blk > 0, 0, BLOCK)
        if sub == nsub - 1:
            hi = col < jnp.where(blk < nb - 1, 3 * BLOCK, 2 * BLOCK)
            edge = hi if edge is None else edge & hi
        s = [lax.dot_general((q_ref[rows, r * HEAD_DIM:(r + 1) * HEAD_DIM] * scale).astype(BF16), kb,
                             (((1,), (1,)), ((), ())), preferred_element_type=F32) + bias[r] for r in heads]
        if edge is not None:
            s = [jnp.where(edge, sr, NEG_INF) for sr in s]
        mx = [jnp.maximum(jnp.max(s[r], axis=-1, keepdims=True), sinks[r]) for r in heads]
        p = [jnp.exp2(s[r] - mx[r]) for r in heads]
        denom = [jnp.sum(p[r], axis=-1, keepdims=True) + jnp.exp2(sinks[r] - mx[r]) for r in heads]
        o = [jnp.dot(p[r].astype(BF16), vb, preferred_element_type=F32) / denom[r] for r in heads]
        for r in heads:
            o_ref[rows, r * HEAD_DIM:(r + 1) * HEAD_DIM] = o[r].astype(o_ref.dtype)


def banded_attention(z, sink, slopes, n_heads):
    b, l, _ = z.shape
    g = n_heads // KV_RATIO
    tq = _pick(l, (512, 256, 128))
    nsub = tq // BLOCK
    nq = l // tq
    nb = l // BLOCK
    qw = KV_RATIO * HEAD_DIM
    kcol = n_heads
    vcol = n_heads + g

    def cur_spec(col0):
        return pl.BlockSpec((None, tq, HEAD_DIM), lambda bi, ni, gi: (bi, ni, col0 + gi))

    def halo_spec(col0, nxt):
        if nxt:
            return pl.BlockSpec((None, BLOCK, HEAD_DIM),
                                lambda bi, ni, gi: (bi, jnp.minimum((ni + 1) * nsub, nb - 1), col0 + gi))
        return pl.BlockSpec((None, BLOCK, HEAD_DIM),
                            lambda bi, ni, gi: (bi, jnp.maximum(ni * nsub - 1, 0), col0 + gi))

    smem = pl.BlockSpec(memory_space=pltpu.SMEM)
    return pl.pallas_call(
        functools.partial(_attn_kernel, nsub=nsub),
        out_shape=jax.ShapeDtypeStruct((b, l, n_heads * HEAD_DIM), BF16),
        grid=(b, nq, g),
        in_specs=[smem, smem,
                  pl.BlockSpec((None, tq, qw), lambda bi, ni, gi: (bi, ni, gi)),
                  halo_spec(kcol, False), cur_spec(kcol), halo_spec(kcol, True),
                  halo_spec(vcol, False), cur_spec(vcol), halo_spec(vcol, True)],
        out_specs=pl.BlockSpec((None, tq, qw), lambda bi, ni, gi: (bi, ni, gi)),
        compiler_params=_params("parallel", "parallel", "arbitrary"),
        name="banded_attention",
    )(sink.astype(F32), slopes, z, z, z, z, z, z, z)


def _split_len(n):
    lg = int(math.log2(n))
    assert 2 ** lg == n
    n2 = 2 ** ((lg + 1) // 2)
    return n // n2, n2


def _cos_sin(n):
    idx = np.arange(n)
    ang = 2.0 * np.pi * ((idx[:, None] * idx[None, :]) % n) / n
    return np.cos(ang), np.sin(ang)


def _twiddle(na, nb, sign):
    n = na * nb
    ang = 2.0 * np.pi * ((np.arange(na)[:, None] * np.arange(nb)[None, :]) % n) / n
    return (jnp.asarray(np.cos(ang)[:, :, None], F32), jnp.asarray(sign * np.sin(ang)[:, :, None], F32))


def _cplx_dft_matrix(n, sign, scale=1.0):
    c, s = _cos_sin(n)
    s = -sign * s
    return np.block([[c, s], [-s, c]]) * scale


def _pack_c(re, im):
    r = lax.bitcast_convert_type(re.astype(BF16).astype(F32), U32)
    i = lax.bitcast_convert_type(im.astype(BF16).astype(F32), U32)
    return r | (i >> 16)


def _unpack_c(p):
    re = lax.bitcast_convert_type(p & jnp.uint32(0xFFFF0000), F32)
    im = lax.bitcast_convert_type(p << 16, F32)
    return jnp.concatenate([re, im], axis=0).astype(BF16)


def _hi_half(p):
    return lax.bitcast_convert_type(p & jnp.uint32(0xFFFF0000), F32)


def _lo_half(p):
    return lax.bitcast_convert_type(p << 16, F32)


def _fft1_kernel(x_ref, s_ref, f_ref, twr_ref, twi_ref, a_ref, *, n2, packed):
    scale = s_ref[...]
    f = f_ref[...]
    xt = pltpu.einshape("rsc->src", x_ref[...])
    outs = []
    for s in range(SUBLANES):
        xs = _hi_half(xt[s]) if packed else xt[s]
        x = (xs * scale).astype(BF16)
        a = jnp.dot(f, x, preferred_element_type=F32)
        ar, ai = a[:n2], a[n2:]
        c, sn = twr_ref[s], twi_ref[s]
        outs.append(_pack_c(ar * c - ai * sn, ar * sn + ai * c))
    a_ref[...] = pltpu.einshape("src->rsc", jnp.stack(outs, axis=0))


def fft_stage1(x, col0, width, n1, n2, rows, scale, kout=None):
    b, _, w = x.shape
    kout = n2 if kout is None else kout
    tc = _pick(math.gcd(math.gcd(w, col0) if col0 else w, width), (512, 256, 128))
    nct = width // tc
    c0 = col0 // tc
    cr, sr = _cos_sin(n2)
    f = jnp.asarray(np.concatenate([cr[:kout], -sr[:kout]], axis=0)[:, :rows], BF16)
    twr, twi = _twiddle(n1, n2, -1.0)
    twr, twi = twr[:, :kout], twi[:, :kout]
    tspec = pl.BlockSpec((SUBLANES, kout, 1), lambda bi, ni, ci: (ni, 0, 0))
    return pl.pallas_call(
        functools.partial(_fft1_kernel, n2=kout, packed=x.dtype == U32),
        out_shape=jax.ShapeDtypeStruct((b, kout, n1, width), U32),
        grid=(b, n1 // SUBLANES, nct),
        in_specs=[pl.BlockSpec((None, rows, SUBLANES, tc), lambda bi, ni, ci: (bi, 0, ni, c0 + ci)),
                  pl.BlockSpec((1, tc), lambda bi, ni, ci: (0, ci)),
                  pl.BlockSpec((2 * kout, rows), lambda bi, ni, ci: (0, 0)),
                  tspec, tspec],
        out_specs=pl.BlockSpec((None, kout, SUBLANES, tc), lambda bi, ni, ci: (bi, 0, ni, ci)),
        compiler_params=_params("parallel", "parallel", "arbitrary"),
        name="fft_stage1",
    )(x.reshape(b, rows, n1, w), scale, f, twr, twi)


def _fourier2_kernel(a_ref, m_ref, cs_ref, o_ref, *, n1, groups):
    outs = []
    for s in range(SUBLANES):
        gm = jnp.dot(m_ref[...], _unpack_c(a_ref[s]), preferred_element_type=F32)
        gr, gi = gm[:n1].astype(BF16), gm[n1:].astype(BF16)
        cols = []
        for q in range(groups):
            sl = slice(q * HEAD_DIM, (q + 1) * HEAD_DIM)
            lhs = jnp.concatenate([gr[:, sl], gi[:, sl]], axis=1)
            cols.append(jnp.dot(lhs, cs_ref[...], preferred_element_type=F32))
        outs.append(jnp.concatenate(cols, axis=1))
    o_ref[...] = pltpu.einshape("src->rsc", jnp.stack(outs, axis=0))


def fourier_mix(z, col0, width):
    b, l, _ = z.shape
    n1, n2 = _split_len(l)
    ones = jnp.ones((1, width), F32)
    a = fft_stage1(z, col0, width, n1, n2, n2, ones)
    m = jnp.asarray(_cplx_dft_matrix(n1, -1.0), BF16)
    cc, sc = _cos_sin(HEAD_DIM)
    cs = jnp.asarray(np.concatenate([cc, sc], axis=0) / math.sqrt(l * HEAD_DIM), BF16)
    groups = width // HEAD_DIM
    out = pl.pallas_call(
        functools.partial(_fourier2_kernel, n1=n1, groups=groups),
        out_shape=jax.ShapeDtypeStruct((b, n1, n2, width), F32),
        grid=(b, n2 // SUBLANES),
        in_specs=[pl.BlockSpec((None, SUBLANES, n1, width), lambda bi, ki: (bi, ki, 0, 0)),
                  pl.BlockSpec((2 * n1, 2 * n1), lambda bi, ki: (0, 0)),
                  pl.BlockSpec((2 * HEAD_DIM, HEAD_DIM), lambda bi, ki: (0, 0))],
        out_specs=pl.BlockSpec((None, n1, SUBLANES, width), lambda bi, ki: (bi, 0, ki, 0)),
        compiler_params=_params("parallel", "arbitrary"),
        name="fourier_stage2",
    )(a, m, cs)
    return out.reshape(b, l, width)


def _conformer_kernel(ap_ref, ac_ref, an_ref, gp_ref, gc_ref, gn_ref, w_ref, b_ref, lg_ref, lb_ref,
                      o_ref, buf_ref, sh_ref, conv_ref, *, tl, width):
    t = pl.program_id(1)
    nt = pl.num_programs(1)
    h = CONV_HALO
    prev = ap_ref[...] * jax.nn.sigmoid(gp_ref[...])
    nxt = an_ref[...] * jax.nn.sigmoid(gn_ref[...])
    buf_ref[0:h, :] = jnp.where(t > 0, prev, 0.0)
    buf_ref[h:h + tl, :] = ac_ref[...] * jax.nn.sigmoid(gc_ref[...])
    buf_ref[h + tl:h + tl + h, :] = jnp.where(t < nt - 1, nxt, 0.0)
    span = tl + 2 * h - SUBLANES
    for ph in range(SUBLANES):
        sh_ref[ph, 0:span, :] = buf_ref[ph:ph + span, :]
    base = h - CONV_KERNEL // 2
    for c in range(width // LANES):
        sl = slice(c * LANES, (c + 1) * LANES)
        acc = jnp.broadcast_to(b_ref[:, sl], (SUBLANES, LANES))[None]
        for j in range(CONV_KERNEL):
            a8, ph = divmod(base + j, SUBLANES)
            wj = jnp.broadcast_to(w_ref[j:j + 1, sl], (SUBLANES, LANES))[None]
            xs = sh_ref[ph, a8 * SUBLANES:a8 * SUBLANES + tl, sl].reshape(tl // SUBLANES, SUBLANES, LANES)
            acc = acc + wj * xs
        conv_ref[:, sl] = acc.reshape(tl, LANES)
    y = conv_ref[...]
    mu = jnp.mean(y, axis=-1, keepdims=True)
    yc = y - mu
    var = jnp.mean(yc * yc, axis=-1, keepdims=True)
    yn = yc * lax.rsqrt(var + EPS) * lg_ref[...] + lb_ref[...]
    o_ref[...] = (yn * jax.nn.sigmoid(yn)).astype(o_ref.dtype)


def conformer_conv(z, width, dw_w, dw_b, ln_g, ln_b):
    b, l, _ = z.shape
    tl = _pick(l, (128, 64, 32, 16))
    h = CONV_HALO
    r = tl // h
    nh = l // h

    def cur(col):
        return pl.BlockSpec((None, tl, width), lambda bi, ti: (bi, ti, col))

    def halo(col, nxt):
        if nxt:
            return pl.BlockSpec((None, h, width), lambda bi, ti: (bi, jnp.minimum((ti + 1) * r, nh - 1), col))
        return pl.BlockSpec((None, h, width), lambda bi, ti: (bi, jnp.maximum(ti * r - 1, 0), col))

    vec = pl.BlockSpec((1, width), lambda bi, ti: (0, 0))
    return pl.pallas_call(
        functools.partial(_conformer_kernel, tl=tl, width=width),
        out_shape=jax.ShapeDtypeStruct((b, l, width), BF16),
        grid=(b, l // tl),
        in_specs=[halo(0, False), cur(0), halo(0, True), halo(1, False), cur(1), halo(1, True),
                  pl.BlockSpec((CONV_KERNEL, width), lambda bi, ti: (0, 0)), vec, vec, vec],
        out_specs=pl.BlockSpec((None, tl, width), lambda bi, ti: (bi, ti, 0)),
        scratch_shapes=[pltpu.VMEM((tl + 2 * h, width), F32),
                        pltpu.VMEM((SUBLANES, tl + 2 * h, width), F32),
                        pltpu.VMEM((tl, width), F32)],
        compiler_params=_params("parallel", "arbitrary"),
        name="conformer_conv",
    )(z, z, z, z, z, z, dw_w, dw_b.reshape(1, width), ln_g.reshape(1, width), ln_b.reshape(1, width))


def _short_conv(prev_ref, cur_ref, next_ref, w_ref, b_ref, first, last, tl):
    x = cur_ref[...]
    rows = lax.broadcasted_iota(jnp.int32, x.shape, 0)
    pr = jnp.where(first, 0.0, prev_ref[SUBLANES - 1:SUBLANES, :])
    nx = jnp.where(last, 0.0, next_ref[0:1, :])
    xm = jnp.where(rows == 0, pr, pltpu.roll(x, 1, axis=0))
    xp = jnp.where(rows == tl - 1, nx, pltpu.roll(x, tl - 1, axis=0))
    return w_ref[0:1, :] * xm + w_ref[1:2, :] * x + w_ref[2:3, :] * xp + b_ref[...]


def _hyena_gate_kernel(*refs, tl):
    x0 = refs[0:3]
    x1 = refs[3:6]
    xv = refs[6:9]
    w0, w1, wv, b0, b1, bv, o_ref = refs[9:]
    t = pl.program_id(1)
    first = t == 0
    last = t == pl.num_programs(1) - 1
    v = _short_conv(*xv, wv, bv, first, last, tl) * _short_conv(*x1, w1, b1, first, last, tl)
    o_ref[...] = _pack_c(v, _short_conv(*x0, w0, b0, first, last, tl))


def hyena_gate(z, col0, width, sw, sb):
    b, l, _ = z.shape
    tl = _pick(l, (512, 256, 128, 64, 32, 16, 8))
    tc = _pick(math.gcd(col0, width), (512, 256, 128))
    r = tl // SUBLANES
    nh = l // SUBLANES
    nct = width // tc

    def trio(part):
        c0 = (col0 + part * width) // tc
        return [pl.BlockSpec((None, SUBLANES, tc), lambda bi, ti, ci: (bi, jnp.maximum(ti * r - 1, 0), c0 + ci)),
                pl.BlockSpec((None, tl, tc), lambda bi, ti, ci: (bi, ti, c0 + ci)),
                pl.BlockSpec((None, SUBLANES, tc), lambda bi, ti, ci: (bi, jnp.minimum((ti + 1) * r, nh - 1), c0 + ci))]

    def wspec(part, rows):
        return pl.BlockSpec((rows, tc), lambda bi, ti, ci: (0, part * nct + ci))

    ospec = pl.BlockSpec((None, tl, tc), lambda bi, ti, ci: (bi, ti, ci))
    sb2 = sb.reshape(1, 3 * width)
    return pl.pallas_call(
        functools.partial(_hyena_gate_kernel, tl=tl),
        out_shape=jax.ShapeDtypeStruct((b, l, width), U32),
        grid=(b, l // tl, nct),
        in_specs=trio(0) + trio(1) + trio(2) + [wspec(0, HYENA_SHORT), wspec(1, HYENA_SHORT), wspec(2, HYENA_SHORT),
                                                 wspec(0, 1), wspec(1, 1), wspec(2, 1)],
        out_specs=ospec,
        compiler_params=_params("parallel", "parallel", "arbitrary"),
        name="hyena_gate",
    )(*([z] * 9), sw, sw, sw, sb2, sb2, sb2)


def _hdot(a, b):
    return jnp.dot(a, b, preferred_element_type=F32, precision=lax.Precision.HIGHEST)


def _filter_kernel(bands_ref, w1t_ref, w1c_ref, w1s_ref, b1_ref, w2_ref, b2_ref, w3_ref, b3_ref, fr_ref,
                   w4_ref, dec_ref, k_ref, norm_ref, *, l, tl):
    i = pl.program_id(0)
    m = i * tl + lax.broadcasted_iota(jnp.int32, (tl, 1), 0)
    j = jnp.where(m < l, m, 2 * l - m).astype(F32)
    t = j / (l - 1.0)
    ang = (2.0 * math.pi / l) * bands_ref[...] * j
    fr = fr_ref[...]
    pre = t * w1t_ref[...] + _hdot(jnp.cos(ang), w1c_ref[...]) - _hdot(jnp.sin(ang), w1s_ref[...])
    hcur = jnp.sin(fr * (pre + b1_ref[...]))
    hcur = jnp.sin(fr * (_hdot(hcur, w2_ref[...]) + b2_ref[...]))
    hcur = jnp.sin(fr * (_hdot(hcur, w3_ref[...]) + b3_ref[...]))
    out = jnp.dot(hcur.astype(BF16), w4_ref[...].astype(BF16), preferred_element_type=F32) * jnp.exp(-t * dec_ref[...])
    out = jnp.where(m == l, 0.0, out)
    k_ref[...] = out

    @pl.when(i == 0)
    def _():
        norm_ref[...] = jnp.zeros_like(norm_ref)

    norm_ref[...] += jnp.sum(jnp.abs(out), axis=0, keepdims=True)


def hyena_filter(l, width, w1, b1, w2, b2, w3, b3, freq, w4):
    hid = w1.shape[1]
    tl = _pick(l, (512, 256, 128, 64, 32, 16, 8))
    nl = l // tl
    bands = jnp.asarray(np.linspace(1e-4, HYENA_BANDS - 1, HYENA_BANDS, dtype=np.float32)[None, :])
    dec = jnp.asarray(np.abs(np.linspace(HYENA_MIN_DECAY, HYENA_MAX_DECAY, width, dtype=np.float32))[None, :])
    full = lambda shape: pl.BlockSpec(shape, lambda i: tuple(0 for _ in shape))
    w4v = w4.reshape(hid, 2, width).transpose(1, 0, 2)
    return pl.pallas_call(
        functools.partial(_filter_kernel, l=l, tl=tl),
        out_shape=(jax.ShapeDtypeStruct((2 * l, width), F32), jax.ShapeDtypeStruct((1, width), F32)),
        grid=(2 * nl,),
        in_specs=[full((1, HYENA_BANDS)), full((1, hid)), full((HYENA_BANDS, hid)), full((HYENA_BANDS, hid)),
                  full((1, hid)), full((hid, hid)), full((1, hid)), full((hid, hid)), full((1, hid)), full((1, hid)),
                  pl.BlockSpec((None, hid, width), lambda i: (i // nl, 0, 0)),
                  full((1, width))],
        out_specs=(pl.BlockSpec((tl, width), lambda i: (i, 0)), full((1, width))),
        compiler_params=_params("arbitrary"),
        name="hyena_filter",
    )(bands, w1[0:1], w1[1:1 + HYENA_BANDS], w1[1 + HYENA_BANDS:], b1.reshape(1, hid), w2, b2.reshape(1, hid),
      w3, b3.reshape(1, hid), freq.reshape(1, hid), w4v, dec)


def _spectrum2_kernel(a_ref, m_ref, k_ref, *, n1, kb):
    for s in range(kb):
        k = jnp.dot(m_ref[...], _unpack_c(a_ref[s]), preferred_element_type=F32)
        k_ref[s] = _pack_c(k[:n1], k[n1:])


MID_BLOCK = 2


def _half_residues(n2):
    return n2 // 2 + SUBLANES


def filter_spectrum(kern, norm, n1, n2):
    nlen, width = kern.shape
    n2h = _half_residues(n2)
    a = fft_stage1(kern.reshape(1, nlen, width), 0, width, n1, n2, n2, 1.0 / norm, kout=n2h)
    m = jnp.asarray(_cplx_dft_matrix(n1, -1.0), BF16)
    kb = MID_BLOCK
    spec = pl.BlockSpec((kb, n1, width), lambda ki: (ki, 0, 0))
    return pl.pallas_call(
        functools.partial(_spectrum2_kernel, n1=n1, kb=kb),
        out_shape=jax.ShapeDtypeStruct((n2h, n1, width), U32),
        grid=(n2h // kb,),
        in_specs=[spec, pl.BlockSpec((2 * n1, 2 * n1), lambda ki: (0, 0))],
        out_specs=spec,
        compiler_params=_params("arbitrary"),
        name="filter_spectrum",
    )(a.reshape(n2h, n1, width), m)


def _hyena_mid_kernel(a_ref, k_ref, mf_ref, mi_ref, twr_ref, twi_ref, b_ref, *, n1, kb):
    for s in range(kb):
        v = jnp.dot(mf_ref[...], _unpack_c(a_ref[s]), preferred_element_type=F32)
        vr, vi = v[:n1], v[n1:]
        kp = k_ref[s]
        kr = lax.bitcast_convert_type(kp & jnp.uint32(0xFFFF0000), F32)
        ki = lax.bitcast_convert_type(kp << 16, F32)
        y = jnp.concatenate([vr * kr - vi * ki, vr * ki + vi * kr], axis=0).astype(BF16)
        bm = jnp.dot(mi_ref[...], y, preferred_element_type=F32)
        br, bi = bm[:n1], bm[n1:]
        c, sn = twr_ref[s], twi_ref[s]
        b_ref[s] = _pack_c(br * c - bi * sn, br * sn + bi * c)


def _hyena_out_kernel(b_ref, f_ref, vx_ref, bias_ref, o_ref):
    bias = bias_ref[...]
    f = f_ref[...]
    bt = pltpu.einshape("rsc->src", b_ref[...])
    ys = [jnp.dot(f, _unpack_c(bt[s]), preferred_element_type=F32) for s in range(SUBLANES)]
    y = pltpu.einshape("src->rsc", jnp.stack(ys, axis=0))
    vx = vx_ref[...]
    o_ref[...] = _lo_half(vx) * (y + _hi_half(vx) * bias)


def hyena_long_conv(v, kspec, bias, n1, n2):
    b, l, width = v.shape
    nlen = n1 * n2
    half = n2 // 2
    n2h = _half_residues(n2)
    ones = jnp.ones((1, width), F32)
    a = fft_stage1(v, 0, width, n1, n2, half, ones, kout=n2h)
    mf = jnp.asarray(_cplx_dft_matrix(n1, -1.0), BF16)
    mi = jnp.asarray(_cplx_dft_matrix(n1, 1.0), BF16)
    twr, twi = _twiddle(n2, n1, 1.0)
    twr, twi = twr[:n2h], twi[:n2h]
    kb = MID_BLOCK
    aspec = pl.BlockSpec((None, kb, n1, width), lambda ki_, bi: (bi, ki_, 0, 0))
    mspec = pl.BlockSpec((2 * n1, 2 * n1), lambda ki_, bi: (0, 0))
    tspec = pl.BlockSpec((kb, n1, 1), lambda ki_, bi: (ki_, 0, 0))
    bm = pl.pallas_call(
        functools.partial(_hyena_mid_kernel, n1=n1, kb=kb),
        out_shape=jax.ShapeDtypeStruct((b, n2h, n1, width), U32),
        grid=(n2h // kb, b),
        in_specs=[aspec, pl.BlockSpec((kb, n1, width), lambda ki_, bi: (ki_, 0, 0)), mspec, mspec, tspec, tspec],
        out_specs=aspec,
        compiler_params=_params("parallel", "arbitrary"),
        name="hyena_mid",
    )(a, kspec, mf, mi, twr, twi)

    tc = _pick(width, (512, 256, 128))
    nct = width // tc
    cr, sr = _cos_sin(n2)
    coef = np.where(np.arange(n2h) > half, 0.0, np.where((np.arange(n2h) == 0) | (np.arange(n2h) == half), 1.0, 2.0))
    finv = jnp.asarray(np.concatenate([cr[:half, :n2h] * coef, -sr[:half, :n2h] * coef], axis=1) / nlen, BF16)
    sv = (b, half, n1, width)
    sspec = pl.BlockSpec((None, half, SUBLANES, tc), lambda bi, ti, ci: (bi, 0, ti, ci))
    out = pl.pallas_call(
        _hyena_out_kernel,
        out_shape=jax.ShapeDtypeStruct(sv, F32),
        grid=(b, n1 // SUBLANES, nct),
        in_specs=[pl.BlockSpec((None, n2h, SUBLANES, tc), lambda bi, ti, ci: (bi, 0, ti, ci)),
                  pl.BlockSpec((half, 2 * n2h), lambda bi, ti, ci: (0, 0)),
                  sspec, pl.BlockSpec((1, tc), lambda bi, ti, ci: (0, ci))],
        out_specs=sspec,
        compiler_params=_params("parallel", "parallel", "arbitrary"),
        name="hyena_out",
    )(bm, finv, v.reshape(sv), bias.reshape(1, width))
    return out.reshape(b, l, width)


def _trunk(x, p, wb):
    b, l, d = x.shape
    m = b * l
    depth = p['mix_norm'].shape[0]
    fourier_w = d // 4
    attn_w = d - fourier_w
    n_heads = attn_w // HEAD_DIM
    kv_w = (n_heads // KV_RATIO) * HEAD_DIM
    conv_w = d // 2
    hy_w = d - conv_w
    slopes = jnp.asarray(alibi_slopes(n_heads))
    hn1, hn2 = _split_len(2 * l)

    xf = x.reshape(m, d)
    xb, r = prep(xf)
    for layer in range(depth):
        i = layer // 2
        if layer % 2 == 0:
            z = proj(xb, r, wb['ab_w_in'], i, F32).reshape(b, l, -1)
            a = banded_attention(z, p['attn_sink'][i], slopes, n_heads)
            f = fourier_mix(z, attn_w + 2 * kv_w, fourier_w)
            parts = [a.reshape(m, attn_w), f.reshape(m, fourier_w)]
            xf, xb, r = out_proj(parts, wb['ab_w_out'], i, xf)
        else:
            z = proj(xb, r, wb['cd_w_in'], i, F32).reshape(b, l, -1)
            c = conformer_conv(z, conv_w, p['conv_dw_w'][i], p['conv_dw_b'][i], p['conv_ln_g'][i], p['conv_ln_b'][i])
            vx = hyena_gate(z, 2 * conv_w, hy_w, p['hy_short_w'][i], p['hy_short_b'][i])
            kern, norm = hyena_filter(l, hy_w, p['hy_filt_w1'][i], p['hy_filt_b1'][i], p['hy_filt_w2'][i],
                                      p['hy_filt_b2'][i], p['hy_filt_w3'][i], p['hy_filt_b3'][i],
                                      p['hy_filt_freq'][i], p['hy_filt_w4'][i])
            kspec = filter_spectrum(kern, norm, hn1, hn2)
            dd = hyena_long_conv(vx, kspec, p['hy_bias'][i], hn1, hn2)
            parts = [c.reshape(m, conv_w), dd.reshape(m, hy_w)]
            xf, xb, r = out_proj(parts, wb['cd_w_out'], i, xf)
        hid = swiglu_in(xb, r, wb['w_gate'], wb['w_up'], layer)
        kh = hid.shape[1] // 2
        assert kh % LANES == 0
        xf = out_proj([(hid, 0, kh)], wb['w_down'], layer, xf, emit_norm=False)
        xf, xb, r = out_proj([(hid, 1, kh)], wb['w_down'], layer, xf, k0=kh)
    return rms_norm(xf, p['final_norm'], F32).reshape(b, l, d)


def kernel(x_prompt, x_sample, mix_norm, ffn_norm, final_norm, w_gate, w_up, w_down, ab_w_in, ab_w_out, attn_sink, cd_w_in, cd_w_out, conv_dw_w, conv_dw_b, conv_ln_g, conv_ln_b, hy_short_w, hy_short_b, hy_filt_w1, hy_filt_b1, hy_filt_w2, hy_filt_b2, hy_filt_w3, hy_filt_b3, hy_filt_freq, hy_filt_w4, hy_bias):
    p = dict(mix_norm=mix_norm, ffn_norm=ffn_norm, final_norm=final_norm, attn_sink=attn_sink,
             conv_dw_w=conv_dw_w, conv_dw_b=conv_dw_b, conv_ln_g=conv_ln_g, conv_ln_b=conv_ln_b,
             hy_short_w=hy_short_w, hy_short_b=hy_short_b, hy_filt_w1=hy_filt_w1, hy_filt_b1=hy_filt_b1,
             hy_filt_w2=hy_filt_w2, hy_filt_b2=hy_filt_b2, hy_filt_w3=hy_filt_w3, hy_filt_b3=hy_filt_b3,
             hy_filt_freq=hy_filt_freq, hy_filt_w4=hy_filt_w4, hy_bias=hy_bias)
    g_ab, g_cd, g_ffn = mix_norm[0::2, :, None], mix_norm[1::2, :, None], ffn_norm[:, :, None]
    wb = dict(w_gate=(g_ffn * w_gate).astype(BF16), w_up=(g_ffn * w_up).astype(BF16), w_down=w_down.astype(BF16),
              ab_w_in=(g_ab * ab_w_in).astype(BF16), ab_w_out=ab_w_out.astype(BF16),
              cd_w_in=(g_cd * cd_w_in).astype(BF16), cd_w_out=cd_w_out.astype(BF16))
    return (_trunk(x_prompt, p, wb), _trunk(x_sample, p, wb))
```

```python
import functools
import math

import numpy as np
import jax
import jax.numpy as jnp
from jax import lax
from jax.experimental import pallas as pl
from jax.experimental.pallas import tpu as pltpu

F32 = jnp.float32
BF16 = jnp.bfloat16
U32 = jnp.uint32

HEAD_DIM = 128
KV_RATIO = 4
WINDOW = 128
BLOCK = 128
CONV_KERNEL = 31
CONV_HALO = 16
HYENA_SHORT = 3
HYENA_BANDS = 16
HYENA_MIN_DECAY = math.log(1e-2) / 1.5
HYENA_MAX_DECAY = math.log(1e-2) / 0.3
EPS = 1e-6
NEG_INF = -1e30

V7X_VMEM_BYTES = 64 * 1024 * 1024
VMEM_LIMIT = V7X_VMEM_BYTES - 8 * 1024 * 1024
LANES = 128
SUBLANES = 8


def _params(*sem):
    return pltpu.CompilerParams(dimension_semantics=sem, vmem_limit_bytes=VMEM_LIMIT)


def _pick(n, candidates):
    for c in candidates:
        if c <= n and n % c == 0:
            return c
    return n


def _rms_kernel(x_ref, g_ref, o_ref):
    x = x_ref[...]
    ms = jnp.mean(x * x, axis=-1, keepdims=True)
    o_ref[...] = (x * lax.rsqrt(ms + EPS) * g_ref[...]).astype(o_ref.dtype)


def rms_norm(x, g, out_dtype):
    m, d = x.shape
    tr = _pick(m, (256, 128, 64, 32, 16, 8))
    return pl.pallas_call(
        _rms_kernel,
        out_shape=jax.ShapeDtypeStruct((m, d), out_dtype),
        grid=(m // tr,),
        in_specs=[pl.BlockSpec((tr, d), lambda i: (i, 0)),
                  pl.BlockSpec((1, d), lambda i: (0, 0))],
        out_specs=pl.BlockSpec((tr, d), lambda i: (i, 0)),
        compiler_params=_params("parallel"),
        name="rms_norm",
    )(x, g.reshape(1, d))


def _prep_kernel(x_ref, xb_ref, r_ref):
    x = x_ref[...]
    xb_ref[...] = x.astype(BF16)
    r_ref[...] = lax.rsqrt(jnp.mean(x * x, axis=-1, keepdims=True) + EPS)


def prep(x):
    m, d = x.shape
    tr = _pick(m, (256, 128, 64, 32, 16, 8))
    return pl.pallas_call(
        _prep_kernel,
        out_shape=(jax.ShapeDtypeStruct((m, d), BF16), jax.ShapeDtypeStruct((m, 1), F32)),
        grid=(m // tr,),
        in_specs=[pl.BlockSpec((tr, d), lambda i: (i, 0))],
        out_specs=(pl.BlockSpec((tr, d), lambda i: (i, 0)), pl.BlockSpec((tr, 1), lambda i: (i, 0))),
        compiler_params=_params("parallel"),
        name="prep",
    )(x)


TILE_VMEM_BUDGET = VMEM_LIMIT - 12 * 1024 * 1024


def _row_spec(tm, k, single=False):
    if single:
        return pl.BlockSpec((tm, k), lambda i, j: (i, 0), pipeline_mode=pl.Buffered(1))
    return pl.BlockSpec((tm, k), lambda i, j: (i, 0))


def _scale_spec(tm):
    return pl.BlockSpec((tm, 1), lambda i, j: (i, 0))


def _proj_kernel(a_ref, r_ref, w_ref, o_ref):
    acc = jnp.dot(a_ref[...], w_ref[...], preferred_element_type=F32)
    o_ref[...] = (acc * r_ref[...]).astype(o_ref.dtype)


def proj(a, r, w, layer, out_dtype):
    m, k = a.shape
    n = w.shape[-1]
    osz = jnp.dtype(out_dtype).itemsize
    tm = _pick(m, (1024, 512, 256, 128))
    tn = next(t for t in (1024, 512, 256, 128)
              if n % t == 0 and 2 * tm * k * 2 + 2 * k * t * 2 + 2 * tm * t * osz <= TILE_VMEM_BUDGET)
    return pl.pallas_call(
        _proj_kernel,
        out_shape=jax.ShapeDtypeStruct((m, n), out_dtype),
        grid=(m // tm, n // tn),
        in_specs=[_row_spec(tm, k), _scale_spec(tm),
                  pl.BlockSpec((None, k, tn), lambda i, j: (layer, 0, j))],
        out_specs=pl.BlockSpec((tm, tn), lambda i, j: (i, j)),
        compiler_params=_params("parallel", "arbitrary"),
        name="proj",
    )(a, r, w)


def _swiglu_kernel(a_ref, r_ref, wg_ref, wu_ref, o_ref):
    a = a_ref[...]
    r = r_ref[...]
    g = jnp.dot(a, wg_ref[...], preferred_element_type=F32) * r
    u = jnp.dot(a, wu_ref[...], preferred_element_type=F32) * r
    o_ref[...] = (g * jax.nn.sigmoid(g) * u).astype(o_ref.dtype)


def _swiglu_tail_kernel(a_ref, r_ref, wg_ref, wu_ref, prev_ref, o_ref):
    del prev_ref
    _swiglu_kernel(a_ref, r_ref, wg_ref, wu_ref, o_ref)


SWIGLU_TN = 768


def swiglu_in(a, r, wg, wu, layer):
    m, k = a.shape
    n = wg.shape[-1]
    tm = _pick(m, (1024, 512, 256, 128))
    tn = SWIGLU_TN if n >= SWIGLU_TN else _pick(n, (256, 128))
    n_main = n - n % tn
    out_shape = jax.ShapeDtypeStruct((m, n), BF16)

    def call(kernel_fn, tn_, col0, ncols, extra_in, extra_specs, alias):
        cb = col0 // tn_
        return pl.pallas_call(
            kernel_fn,
            out_shape=out_shape,
            grid=(m // tm, ncols // tn_),
            in_specs=[_row_spec(tm, k), _scale_spec(tm),
                      pl.BlockSpec((None, k, tn_), lambda i, j: (layer, 0, cb + j)),
                      pl.BlockSpec((None, k, tn_), lambda i, j: (layer, 0, cb + j))] + extra_specs,
            out_specs=pl.BlockSpec((tm, tn_), lambda i, j: (i, cb + j)),
            input_output_aliases=alias,
            compiler_params=_params("parallel", "arbitrary"),
            name="swiglu_in",
        )(a, r, wg, wu, *extra_in)

    hid = call(_swiglu_kernel, tn, 0, n_main, [], [], {})
    if n_main < n:
        tail = n - n_main
        assert tail % LANES == 0 and n_main % tail == 0
        hid = call(_swiglu_tail_kernel, tail, n_main, tail, [hid],
                   [pl.BlockSpec(memory_space=pl.ANY)], {4: 0})
    return hid


def _out_kernel(*refs, dtypes, d_model, emit_norm):
    n_parts = len(dtypes)
    a_refs = refs[:n_parts]
    w_refs = refs[n_parts:2 * n_parts]
    res_ref = refs[2 * n_parts]
    if emit_norm:
        o_ref, ob_ref, r_ref, ss_ref = refs[2 * n_parts + 1:2 * n_parts + 5]
        cast_refs = list(refs[2 * n_parts + 5:])
    else:
        o_ref = refs[2 * n_parts + 1]
        cast_refs = list(refs[2 * n_parts + 2:])
    j = pl.program_id(1)

    @pl.when(j == 0)
    def _():
        if emit_norm:
            ss_ref[...] = jnp.zeros_like(ss_ref)
        k = 0
        for a_ref, dt in zip(a_refs, dtypes):
            if dt != BF16:
                cast_refs[k][...] = a_ref[...].astype(BF16)
                k += 1

    acc = res_ref[...]
    k = 0
    for a_ref, w_ref, dt in zip(a_refs, w_refs, dtypes):
        if dt != BF16:
            a = cast_refs[k][...]
            k += 1
        else:
            a = a_ref[...]
        acc = acc + jnp.dot(a, w_ref[...], preferred_element_type=F32)
    o_ref[...] = acc
    if emit_norm:
        ob_ref[...] = acc.astype(BF16)
        ss_ref[...] += jnp.sum(acc * acc, axis=-1, keepdims=True)

        @pl.when(j == pl.num_programs(1) - 1)
        def _():
            r_ref[...] = lax.rsqrt(ss_ref[...] * (1.0 / d_model) + EPS)


def out_proj(parts, w, layer, res, k0=0, emit_norm=True):
    parts = [p if isinstance(p, tuple) else (p, 0, p.shape[1]) for p in parts]
    m = res.shape[0]
    n = w.shape[-1]
    kp = [k for _, _, k in parts]
    dtypes = tuple(a.dtype for a, _, _ in parts)
    out_bytes = 4 + 4 + 2 if emit_norm else 4 + 4

    def fits(tm_, t, nbuf):
        tile_bytes = sum(tm_ * k * a.dtype.itemsize for a, _, k in parts)
        cast_bytes = sum(tm_ * k * 2 for k, dt in zip(kp, dtypes) if dt != BF16)
        return m % tm_ == 0 and n % t == 0 and (nbuf * tile_bytes + cast_bytes + 2 * sum(kp) * t * 2
                                                + 2 * tm_ * t * out_bytes) <= TILE_VMEM_BUDGET

    shapes = [(1024, 1024), (1024, 512), (512, 1024), (1024, 256), (512, 512), (512, 256), (256, 256), (128, 128)]
    shapes = [(a_, b_) for a_, b_ in shapes if m % a_ == 0 and n % b_ == 0] + [(_pick(m, (128, 64, 32, 16, 8)), _pick(n, (128,)))]
    order = [(a_, b_, 2) for a_, b_ in shapes[:3]] + [(a_, b_, nb) for a_, b_ in shapes for nb in (2, 1)]
    tm, tn, nbuf = next(c for c in order if fits(*c))
    in_specs = []
    for _, cb, k in parts:
        if nbuf == 1:
            in_specs.append(pl.BlockSpec((tm, k), lambda i, j, cb=cb: (i, cb), pipeline_mode=pl.Buffered(1)))
        else:
            in_specs.append(pl.BlockSpec((tm, k), lambda i, j, cb=cb: (i, cb)))
    off = k0
    for k in kp:
        assert off % k == 0, "each part must start at a multiple of its own width"
        blk = off // k
        in_specs.append(pl.BlockSpec((None, k, tn), lambda i, j, blk=blk: (layer, blk, j)))
        off += k
    tile = pl.BlockSpec((tm, tn), lambda i, j: (i, j))
    in_specs.append(tile)
    scratch = [pltpu.VMEM((tm, k), BF16) for k, dt in zip(kp, dtypes) if dt != BF16]
    if emit_norm:
        out_shape = (jax.ShapeDtypeStruct((m, n), F32), jax.ShapeDtypeStruct((m, n), BF16),
                     jax.ShapeDtypeStruct((m, 1), F32))
        out_specs = (tile, tile, _scale_spec(tm))
        scratch = [pltpu.VMEM((tm, 1), F32)] + scratch
    else:
        out_shape = jax.ShapeDtypeStruct((m, n), F32)
        out_specs = tile
    return pl.pallas_call(
        functools.partial(_out_kernel, dtypes=dtypes, d_model=n, emit_norm=emit_norm),
        out_shape=out_shape,
        grid=(m // tm, n // tn),
        in_specs=in_specs,
        out_specs=out_specs,
        scratch_shapes=scratch,
        compiler_params=_params("parallel", "arbitrary"),
        name="out_proj",
    )(*[a for a, _, _ in parts], *([w] * len(parts)), res)


def alibi_slopes(n):
    def pow2(m):
        start = 2.0 ** (-(2.0 ** -(math.log2(m) - 3)))
        return [start ** (i + 1) for i in range(m)]
    if math.log2(n).is_integer():
        s = pow2(n)
    else:
        c = 2 ** math.floor(math.log2(n))
        s = pow2(c) + pow2(2 * c)[0::2][: n - c]
    return np.asarray(s, np.float32)


def _attn_kernel(sink_ref, slope_ref, q_ref, kp_ref, kc_ref, kn_ref, vp_ref, vc_ref, vn_ref, o_ref, *, nsub):
    n = pl.program_id(1)
    g = pl.program_id(2)
    nb = pl.num_programs(1) * nsub
    k = jnp.concatenate([kp_ref[...], kc_ref[...], kn_ref[...]], axis=0).astype(BF16)
    v = jnp.concatenate([vp_ref[...], vc_ref[...], vn_ref[...]], axis=0).astype(BF16)
    row = lax.broadcasted_iota(jnp.int32, (BLOCK, 3 * BLOCK), 0)
    col = lax.broadcasted_iota(jnp.int32, (BLOCK, 3 * BLOCK), 1)
    dist = jnp.abs(col - BLOCK - row)
    band = dist <= WINDOW
    distf = dist.astype(F32)
    log2e = math.log2(math.e)
    scale = HEAD_DIM ** -0.5 * log2e
    heads = range(KV_RATIO)
    sinks = [sink_ref[g * KV_RATIO + r] * log2e for r in heads]
    bias = [jnp.where(band, -(slope_ref[g * KV_RATIO + r] * log2e) * distf, NEG_INF) for r in heads]
    for sub in range(nsub):
        blk = n * nsub + sub
        rows = slice(sub * BLOCK, (sub + 1) * BLOCK)
        keys = slice(sub * BLOCK, (sub + 3) * BLOCK)
        kb, vb = k[keys], v[keys]
        edge = None
        if sub == 0:
            edge = col >= jnp.where(blk > 0, 0, BLOCK)
        if sub == nsub - 1:
            hi = col < jnp.where(blk < nb - 1, 3 * BLOCK, 2 * BLOCK)
            edge = hi if edge is None else edge & hi
        s = [lax.dot_general((q_ref[rows, r * HEAD_DIM:(r + 1) * HEAD_DIM] * scale).astype(BF16), kb,
                             (((1,), (1,)), ((), ())), preferred_element_type=F32) + bias[r] for r in heads]
        if edge is not None:
            s = [jnp.where(edge, sr, NEG_INF) for sr in s]
        mx = [jnp.maximum(jnp.max(s[r], axis=-1, keepdims=True), sinks[r]) for r in heads]
        p = [jnp.exp2(s[r] - mx[r]) for r in heads]
        denom = [jnp.sum(p[r], axis=-1, keepdims=True) + jnp.exp2(sinks[r] - mx[r]) for r in heads]
        o = [jnp.dot(p[r].astype(BF16), vb, preferred_element_type=F32) / denom[r] for r in heads]
        for r in heads:
            o_ref[rows, r * HEAD_DIM:(r + 1) * HEAD_DIM] = o[r].astype(o_ref.dtype)


def banded_attention(z, sink, slopes, n_heads):
    b, l, _ = z.shape
    g = n_heads // KV_RATIO
    tq = _pick(l, (512, 256, 128))
    nsub = tq // BLOCK
    nq = l // tq
    nb = l // BLOCK
    qw = KV_RATIO * HEAD_DIM
    kcol = n_heads
    vcol = n_heads + g

    def cur_spec(col0):
        return pl.BlockSpec((None, tq, HEAD_DIM), lambda bi, ni, gi: (bi, ni, col0 + gi))

    def halo_spec(col0, nxt):
        if nxt:
            return pl.BlockSpec((None, BLOCK, HEAD_DIM),
                                lambda bi, ni, gi: (bi, jnp.minimum((ni + 1) * nsub, nb - 1), col0 + gi))
        return pl.BlockSpec((None, BLOCK, HEAD_DIM),
                            lambda bi, ni, gi: (bi, jnp.maximum(ni * nsub - 1, 0), col0 + gi))

    smem = pl.BlockSpec(memory_space=pltpu.SMEM)
    return pl.pallas_call(
        functools.partial(_attn_kernel, nsub=nsub),
        out_shape=jax.ShapeDtypeStruct((b, l, n_heads * HEAD_DIM), BF16),
        grid=(b, nq, g),
        in_specs=[smem, smem,
                  pl.BlockSpec((None, tq, qw), lambda bi, ni, gi: (bi, ni, gi)),
                  halo_spec(kcol, False), cur_spec(kcol), halo_spec(kcol, True),
                  halo_spec(vcol, False), cur_spec(vcol), halo_spec(vcol, True)],
        out_specs=pl.BlockSpec((None, tq, qw), lambda bi, ni, gi: (bi, ni, gi)),
        compiler_params=_params("parallel", "parallel", "arbitrary"),
        name="banded_attention",
    )(sink.astype(F32), slopes, z, z, z, z, z, z, z)


def _split_len(n):
    lg = int(math.log2(n))
    assert 2 ** lg == n
    n2 = 2 ** ((lg + 1) // 2)
    return n // n2, n2


def _cos_sin(n):
    idx = np.arange(n)
    ang = 2.0 * np.pi * ((idx[:, None] * idx[None, :]) % n) / n
    return np.cos(ang), np.sin(ang)


def _twiddle(na, nb, sign):
    n = na * nb
    ang = 2.0 * np.pi * ((np.arange(na)[:, None] * np.arange(nb)[None, :]) % n) / n
    return (jnp.asarray(np.cos(ang)[:, :, None], F32), jnp.asarray(sign * np.sin(ang)[:, :, None], F32))


def _cplx_dft_matrix(n, sign, scale=1.0):
    c, s = _cos_sin(n)
    s = -sign * s
    return np.block([[c, s], [-s, c]]) * scale


def _pack_c(re, im):
    r = lax.bitcast_convert_type(re.astype(BF16).astype(F32), U32)
    i = lax.bitcast_convert_type(im.astype(BF16).astype(F32), U32)
    return r | (i >> 16)


def _unpack_c(p):
    re = lax.bitcast_convert_type(p & jnp.uint32(0xFFFF0000), F32)
    im = lax.bitcast_convert_type(p << 16, F32)
    return jnp.concatenate([re, im], axis=0).astype(BF16)


def _fft1_kernel(x_ref, s_ref, f_ref, twr_ref, twi_ref, a_ref, *, n2):
    scale = s_ref[...]
    f = f_ref[...]
    xt = pltpu.einshape("rsc->src", x_ref[...])
    outs = []
    for s in range(SUBLANES):
        x = (xt[s] * scale).astype(BF16)
        a = jnp.dot(f, x, preferred_element_type=F32)
        ar, ai = a[:n2], a[n2:]
        c, sn = twr_ref[s], twi_ref[s]
        outs.append(_pack_c(ar * c - ai * sn, ar * sn + ai * c))
    a_ref[...] = pltpu.einshape("src->rsc", jnp.stack(outs, axis=0))


def fft_stage1(x, col0, width, n1, n2, rows, scale, kout=None):
    b, _, w = x.shape
    kout = n2 if kout is None else kout
    tc = _pick(math.gcd(math.gcd(w, col0) if col0 else w, width), (512, 256, 128))
    nct = width // tc
    c0 = col0 // tc
    cr, sr = _cos_sin(n2)
    f = jnp.asarray(np.concatenate([cr[:kout], -sr[:kout]], axis=0)[:, :rows], BF16)
    twr, twi = _twiddle(n1, n2, -1.0)
    twr, twi = twr[:, :kout], twi[:, :kout]
    tspec = pl.BlockSpec((SUBLANES, kout, 1), lambda bi, ni, ci: (ni, 0, 0))
    return pl.pallas_call(
        functools.partial(_fft1_kernel, n2=kout),
        out_shape=jax.ShapeDtypeStruct((b, kout, n1, width), U32),
        grid=(b, n1 // SUBLANES, nct),
        in_specs=[pl.BlockSpec((None, rows, SUBLANES, tc), lambda bi, ni, ci: (bi, 0, ni, c0 + ci)),
                  pl.BlockSpec((1, tc), lambda bi, ni, ci: (0, ci)),
                  pl.BlockSpec((2 * kout, rows), lambda bi, ni, ci: (0, 0)),
                  tspec, tspec],
        out_specs=pl.BlockSpec((None, kout, SUBLANES, tc), lambda bi, ni, ci: (bi, 0, ni, ci)),
        compiler_params=_params("parallel", "parallel", "arbitrary"),
        name="fft_stage1",
    )(x.reshape(b, rows, n1, w), scale, f, twr, twi)


def _fourier2_kernel(a_ref, m_ref, cs_ref, o_ref, *, n1, groups):
    outs = []
    for s in range(SUBLANES):
        gm = jnp.dot(m_ref[...], _unpack_c(a_ref[s]), preferred_element_type=F32)
        gr, gi = gm[:n1].astype(BF16), gm[n1:].astype(BF16)
        cols = []
        for q in range(groups):
            sl = slice(q * HEAD_DIM, (q + 1) * HEAD_DIM)
            lhs = jnp.concatenate([gr[:, sl], gi[:, sl]], axis=1)
            cols.append(jnp.dot(lhs, cs_ref[...], preferred_element_type=F32))
        outs.append(jnp.concatenate(cols, axis=1))
    o_ref[...] = pltpu.einshape("src->rsc", jnp.stack(outs, axis=0))


def fourier_mix(z, col0, width):
    b, l, _ = z.shape
    n1, n2 = _split_len(l)
    ones = jnp.ones((1, width), F32)
    a = fft_stage1(z, col0, width, n1, n2, n2, ones)
    m = jnp.asarray(_cplx_dft_matrix(n1, -1.0), BF16)
    cc, sc = _cos_sin(HEAD_DIM)
    cs = jnp.asarray(np.concatenate([cc, sc], axis=0) / math.sqrt(l * HEAD_DIM), BF16)
    groups = width // HEAD_DIM
    out = pl.pallas_call(
        functools.partial(_fourier2_kernel, n1=n1, groups=groups),
        out_shape=jax.ShapeDtypeStruct((b, n1, n2, width), F32),
        grid=(b, n2 // SUBLANES),
        in_specs=[pl.BlockSpec((None, SUBLANES, n1, width), lambda bi, ki: (bi, ki, 0, 0)),
                  pl.BlockSpec((2 * n1, 2 * n1), lambda bi, ki: (0, 0)),
                  pl.BlockSpec((2 * HEAD_DIM, HEAD_DIM), lambda bi, ki: (0, 0))],
        out_specs=pl.BlockSpec((None, n1, SUBLANES, width), lambda bi, ki: (bi, 0, ki, 0)),
        compiler_params=_params("parallel", "arbitrary"),
        name="fourier_stage2",
    )(a, m, cs)
    return out.reshape(b, l, width)


def _conformer_kernel(ap_ref, ac_ref, an_ref, gp_ref, gc_ref, gn_ref, w_ref, b_ref, lg_ref, lb_ref,
                      o_ref, buf_ref, sh_ref, conv_ref, *, tl, width):
    t = pl.program_id(1)
    nt = pl.num_programs(1)
    h = CONV_HALO
    prev = ap_ref[...] * jax.nn.sigmoid(gp_ref[...])
    nxt = an_ref[...] * jax.nn.sigmoid(gn_ref[...])
    buf_ref[0:h, :] = jnp.where(t > 0, prev, 0.0)
    buf_ref[h:h + tl, :] = ac_ref[...] * jax.nn.sigmoid(gc_ref[...])
    buf_ref[h + tl:h + tl + h, :] = jnp.where(t < nt - 1, nxt, 0.0)
    span = tl + 2 * h - SUBLANES
    for ph in range(SUBLANES):
        sh_ref[ph, 0:span, :] = buf_ref[ph:ph + span, :]
    base = h - CONV_KERNEL // 2
    for c in range(width // LANES):
        sl = slice(c * LANES, (c + 1) * LANES)
        acc = jnp.broadcast_to(b_ref[:, sl], (SUBLANES, LANES))[None]
        for j in range(CONV_KERNEL):
            a8, ph = divmod(base + j, SUBLANES)
            wj = jnp.broadcast_to(w_ref[j:j + 1, sl], (SUBLANES, LANES))[None]
            xs = sh_ref[ph, a8 * SUBLANES:a8 * SUBLANES + tl, sl].reshape(tl // SUBLANES, SUBLANES, LANES)
            acc = acc + wj * xs
        conv_ref[:, sl] = acc.reshape(tl, LANES)
    y = conv_ref[...]
    mu = jnp.mean(y, axis=-1, keepdims=True)
    yc = y - mu
    var = jnp.mean(yc * yc, axis=-1, keepdims=True)
    yn = yc * lax.rsqrt(var + EPS) * lg_ref[...] + lb_ref[...]
    o_ref[...] = (yn * jax.nn.sigmoid(yn)).astype(o_ref.dtype)


def conformer_conv(z, width, dw_w, dw_b, ln_g, ln_b):
    b, l, _ = z.shape
    tl = _pick(l, (128, 64, 32, 16))
    h = CONV_HALO
    r = tl // h
    nh = l // h

    def cur(col):
        return pl.BlockSpec((None, tl, width), lambda bi, ti: (bi, ti, col))

    def halo(col, nxt):
        if nxt:
            return pl.BlockSpec((None, h, width), lambda bi, ti: (bi, jnp.minimum((ti + 1) * r, nh - 1), col))
        return pl.BlockSpec((None, h, width), lambda bi, ti: (bi, jnp.maximum(ti * r - 1, 0), col))

    vec = pl.BlockSpec((1, width), lambda bi, ti: (0, 0))
    return pl.pallas_call(
        functools.partial(_conformer_kernel, tl=tl, width=width),
        out_shape=jax.ShapeDtypeStruct((b, l, width), BF16),
        grid=(b, l // tl),
        in_specs=[halo(0, False), cur(0), halo(0, True), halo(1, False), cur(1), halo(1, True),
                  pl.BlockSpec((CONV_KERNEL, width), lambda bi, ti: (0, 0)), vec, vec, vec],
        out_specs=pl.BlockSpec((None, tl, width), lambda bi, ti: (bi, ti, 0)),
        scratch_shapes=[pltpu.VMEM((tl + 2 * h, width), F32),
                        pltpu.VMEM((SUBLANES, tl + 2 * h, width), F32),
                        pltpu.VMEM((tl, width), F32)],
        compiler_params=_params("parallel", "arbitrary"),
        name="conformer_conv",
    )(z, z, z, z, z, z, dw_w, dw_b.reshape(1, width), ln_g.reshape(1, width), ln_b.reshape(1, width))


def _short_conv(prev_ref, cur_ref, next_ref, w_ref, b_ref, first, last, tl):
    x = cur_ref[...]
    rows = lax.broadcasted_iota(jnp.int32, x.shape, 0)
    pr = jnp.where(first, 0.0, prev_ref[SUBLANES - 1:SUBLANES, :])
    nx = jnp.where(last, 0.0, next_ref[0:1, :])
    xm = jnp.where(rows == 0, pr, pltpu.roll(x, 1, axis=0))
    xp = jnp.where(rows == tl - 1, nx, pltpu.roll(x, tl - 1, axis=0))
    return w_ref[0:1, :] * xm + w_ref[1:2, :] * x + w_ref[2:3, :] * xp + b_ref[...]


def _hyena_gate_kernel(*refs, tl):
    x0 = refs[0:3]
    x1 = refs[3:6]
    xv = refs[6:9]
    w0, w1, wv, b0, b1, bv, x0_out, v_out = refs[9:]
    t = pl.program_id(1)
    first = t == 0
    last = t == pl.num_programs(1) - 1
    x0_out[...] = _short_conv(*x0, w0, b0, first, last, tl)
    v_out[...] = _short_conv(*xv, wv, bv, first, last, tl) * _short_conv(*x1, w1, b1, first, last, tl)


def hyena_gate(z, col0, width, sw, sb):
    b, l, _ = z.shape
    tl = _pick(l, (512, 256, 128, 64, 32, 16, 8))
    tc = _pick(math.gcd(col0, width), (512, 256, 128))
    r = tl // SUBLANES
    nh = l // SUBLANES
    nct = width // tc

    def trio(part):
        c0 = (col0 + part * width) // tc
        return [pl.BlockSpec((None, SUBLANES, tc), lambda bi, ti, ci: (bi, jnp.maximum(ti * r - 1, 0), c0 + ci)),
                pl.BlockSpec((None, tl, tc), lambda bi, ti, ci: (bi, ti, c0 + ci)),
                pl.BlockSpec((None, SUBLANES, tc), lambda bi, ti, ci: (bi, jnp.minimum((ti + 1) * r, nh - 1), c0 + ci))]

    def wspec(part, rows):
        return pl.BlockSpec((rows, tc), lambda bi, ti, ci: (0, part * nct + ci))

    out = jax.ShapeDtypeStruct((b, l, width), F32)
    ospec = pl.BlockSpec((None, tl, tc), lambda bi, ti, ci: (bi, ti, ci))
    sb2 = sb.reshape(1, 3 * width)
    return pl.pallas_call(
        functools.partial(_hyena_gate_kernel, tl=tl),
        out_shape=(out, out),
        grid=(b, l // tl, nct),
        in_specs=trio(0) + trio(1) + trio(2) + [wspec(0, HYENA_SHORT), wspec(1, HYENA_SHORT), wspec(2, HYENA_SHORT),
                                                 wspec(0, 1), wspec(1, 1), wspec(2, 1)],
        out_specs=(ospec, ospec),
        compiler_params=_params("parallel", "parallel", "arbitrary"),
        name="hyena_gate",
    )(*([z] * 9), sw, sw, sw, sb2, sb2, sb2)


def _hdot(a, b):
    return jnp.dot(a, b, preferred_element_type=F32, precision=lax.Precision.HIGHEST)


def _filter_kernel(bands_ref, w1t_ref, w1c_ref, w1s_ref, b1_ref, w2_ref, b2_ref, w3_ref, b3_ref, fr_ref,
                   w4_ref, dec_ref, k_ref, norm_ref, *, l, tl):
    i = pl.program_id(0)
    m = i * tl + lax.broadcasted_iota(jnp.int32, (tl, 1), 0)
    j = jnp.where(m < l, m, 2 * l - m).astype(F32)
    t = j / (l - 1.0)
    ang = (2.0 * math.pi / l) * bands_ref[...] * j
    fr = fr_ref[...]
    pre = t * w1t_ref[...] + _hdot(jnp.cos(ang), w1c_ref[...]) - _hdot(jnp.sin(ang), w1s_ref[...])
    hcur = jnp.sin(fr * (pre + b1_ref[...]))
    hcur = jnp.sin(fr * (_hdot(hcur, w2_ref[...]) + b2_ref[...]))
    hcur = jnp.sin(fr * (_hdot(hcur, w3_ref[...]) + b3_ref[...]))
    out = jnp.dot(hcur.astype(BF16), w4_ref[...].astype(BF16), preferred_element_type=F32) * jnp.exp(-t * dec_ref[...])
    out = jnp.where(m == l, 0.0, out)
    k_ref[...] = out

    @pl.when(i == 0)
    def _():
        norm_ref[...] = jnp.zeros_like(norm_ref)

    norm_ref[...] += jnp.sum(jnp.abs(out), axis=0, keepdims=True)


def hyena_filter(l, width, w1, b1, w2, b2, w3, b3, freq, w4):
    hid = w1.shape[1]
    tl = _pick(l, (512, 256, 128, 64, 32, 16, 8))
    nl = l // tl
    bands = jnp.asarray(np.linspace(1e-4, HYENA_BANDS - 1, HYENA_BANDS, dtype=np.float32)[None, :])
    dec = jnp.asarray(np.abs(np.linspace(HYENA_MIN_DECAY, HYENA_MAX_DECAY, width, dtype=np.float32))[None, :])
    full = lambda shape: pl.BlockSpec(shape, lambda i: tuple(0 for _ in shape))
    w4v = w4.reshape(hid, 2, width).transpose(1, 0, 2)
    return pl.pallas_call(
        functools.partial(_filter_kernel, l=l, tl=tl),
        out_shape=(jax.ShapeDtypeStruct((2 * l, width), F32), jax.ShapeDtypeStruct((1, width), F32)),
        grid=(2 * nl,),
        in_specs=[full((1, HYENA_BANDS)), full((1, hid)), full((HYENA_BANDS, hid)), full((HYENA_BANDS, hid)),
                  full((1, hid)), full((hid, hid)), full((1, hid)), full((hid, hid)), full((1, hid)), full((1, hid)),
                  pl.BlockSpec((None, hid, width), lambda i: (i // nl, 0, 0)),
                  full((1, width))],
        out_specs=(pl.BlockSpec((tl, width), lambda i: (i, 0)), full((1, width))),
        compiler_params=_params("arbitrary"),
        name="hyena_filter",
    )(bands, w1[0:1], w1[1:1 + HYENA_BANDS], w1[1 + HYENA_BANDS:], b1.reshape(1, hid), w2, b2.reshape(1, hid),
      w3, b3.reshape(1, hid), freq.reshape(1, hid), w4v, dec)


def _spectrum2_kernel(a_ref, m_ref, k_ref, *, n1, kb):
    for s in range(kb):
        k = jnp.dot(m_ref[...], _unpack_c(a_ref[s]), preferred_element_type=F32)
        k_ref[s] = _pack_c(k[:n1], k[n1:])


MID_BLOCK = 2


def _half_residues(n2):
    return n2 // 2 + SUBLANES


def filter_spectrum(kern, norm, n1, n2):
    nlen, width = kern.shape
    n2h = _half_residues(n2)
    a = fft_stage1(kern.reshape(1, nlen, width), 0, width, n1, n2, n2, 1.0 / norm, kout=n2h)
    m = jnp.asarray(_cplx_dft_matrix(n1, -1.0), BF16)
    kb = MID_BLOCK
    spec = pl.BlockSpec((kb, n1, width), lambda ki: (ki, 0, 0))
    return pl.pallas_call(
        functools.partial(_spectrum2_kernel, n1=n1, kb=kb),
        out_shape=jax.ShapeDtypeStruct((n2h, n1, width), U32),
        grid=(n2h // kb,),
        in_specs=[spec, pl.BlockSpec((2 * n1, 2 * n1), lambda ki: (0, 0))],
        out_specs=spec,
        compiler_params=_params("arbitrary"),
        name="filter_spectrum",
    )(a.reshape(n2h, n1, width), m)


def _hyena_mid_kernel(a_ref, k_ref, mf_ref, mi_ref, twr_ref, twi_ref, b_ref, *, n1, kb):
    for s in range(kb):
        v = jnp.dot(mf_ref[...], _unpack_c(a_ref[s]), preferred_element_type=F32)
        vr, vi = v[:n1], v[n1:]
        kp = k_ref[s]
        kr = lax.bitcast_convert_type(kp & jnp.uint32(0xFFFF0000), F32)
        ki = lax.bitcast_convert_type(kp << 16, F32)
        y = jnp.concatenate([vr * kr - vi * ki, vr * ki + vi * kr], axis=0).astype(BF16)
        bm = jnp.dot(mi_ref[...], y, preferred_element_type=F32)
        br, bi = bm[:n1], bm[n1:]
        c, sn = twr_ref[s], twi_ref[s]
        b_ref[s] = _pack_c(br * c - bi * sn, br * sn + bi * c)


def _hyena_out_kernel(b_ref, f_ref, v_ref, x0_ref, bias_ref, o_ref):
    bias = bias_ref[...]
    f = f_ref[...]
    bt = pltpu.einshape("rsc->src", b_ref[...])
    ys = [jnp.dot(f, _unpack_c(bt[s]), preferred_element_type=F32) for s in range(SUBLANES)]
    y = pltpu.einshape("src->rsc", jnp.stack(ys, axis=0))
    o_ref[...] = x0_ref[...] * (y + v_ref[...] * bias)


def hyena_long_conv(v, x0, kspec, bias, n1, n2):
    b, l, width = v.shape
    nlen = n1 * n2
    half = n2 // 2
    n2h = _half_residues(n2)
    ones = jnp.ones((1, width), F32)
    a = fft_stage1(v, 0, width, n1, n2, half, ones, kout=n2h)
    mf = jnp.asarray(_cplx_dft_matrix(n1, -1.0), BF16)
    mi = jnp.asarray(_cplx_dft_matrix(n1, 1.0), BF16)
    twr, twi = _twiddle(n2, n1, 1.0)
    twr, twi = twr[:n2h], twi[:n2h]
    kb = MID_BLOCK
    aspec = pl.BlockSpec((None, kb, n1, width), lambda ki_, bi: (bi, ki_, 0, 0))
    mspec = pl.BlockSpec((2 * n1, 2 * n1), lambda ki_, bi: (0, 0))
    tspec = pl.BlockSpec((kb, n1, 1), lambda ki_, bi: (ki_, 0, 0))
    bm = pl.pallas_call(
        functools.partial(_hyena_mid_kernel, n1=n1, kb=kb),
        out_shape=jax.ShapeDtypeStruct((b, n2h, n1, width), U32),
        grid=(n2h // kb, b),
        in_specs=[aspec, pl.BlockSpec((kb, n1, width), lambda ki_, bi: (ki_, 0, 0)), mspec, mspec, tspec, tspec],
        out_specs=aspec,
        compiler_params=_params("parallel", "arbitrary"),
        name="hyena_mid",
    )(a, kspec, mf, mi, twr, twi)

    tc = _pick(width, (512, 256, 128))
    nct = width // tc
    cr, sr = _cos_sin(n2)
    coef = np.where(np.arange(n2h) > half, 0.0, np.where((np.arange(n2h) == 0) | (np.arange(n2h) == half), 1.0, 2.0))
    finv = jnp.asarray(np.concatenate([cr[:half, :n2h] * coef, -sr[:half, :n2h] * coef], axis=1) / nlen, BF16)
    sv = (b, half, n1, width)
    sspec = pl.BlockSpec((None, half, SUBLANES, tc), lambda bi, ti, ci: (bi, 0, ti, ci))
    out = pl.pallas_call(
        _hyena_out_kernel,
        out_shape=jax.ShapeDtypeStruct(sv, F32),
        grid=(b, n1 // SUBLANES, nct),
        in_specs=[pl.BlockSpec((None, n2h, SUBLANES, tc), lambda bi, ti, ci: (bi, 0, ti, ci)),
                  pl.BlockSpec((half, 2 * n2h), lambda bi, ti, ci: (0, 0)),
                  sspec, sspec, pl.BlockSpec((1, tc), lambda bi, ti, ci: (0, ci))],
        out_specs=sspec,
        compiler_params=_params("parallel", "parallel", "arbitrary"),
        name="hyena_out",
    )(bm, finv, v.reshape(sv), x0.reshape(sv), bias.reshape(1, width))
    return out.reshape(b, l, width)


def _trunk(x, p, wb):
    b, l, d = x.shape
    m = b * l
    depth = p['mix_norm'].shape[0]
    fourier_w = d // 4
    attn_w = d - fourier_w
    n_heads = attn_w // HEAD_DIM
    kv_w = (n_heads // KV_RATIO) * HEAD_DIM
    conv_w = d // 2
    hy_w = d - conv_w
    slopes = jnp.asarray(alibi_slopes(n_heads))
    hn1, hn2 = _split_len(2 * l)

    xf = x.reshape(m, d)
    xb, r = prep(xf)
    for layer in range(depth):
        i = layer // 2
        if layer % 2 == 0:
            z = proj(xb, r, wb['ab_w_in'], i, F32).reshape(b, l, -1)
            a = banded_attention(z, p['attn_sink'][i], slopes, n_heads)
            f = fourier_mix(z, attn_w + 2 * kv_w, fourier_w)
            parts = [a.reshape(m, attn_w), f.reshape(m, fourier_w)]
            xf, xb, r = out_proj(parts, wb['ab_w_out'], i, xf)
        else:
            z = proj(xb, r, wb['cd_w_in'], i, F32).reshape(b, l, -1)
            c = conformer_conv(z, conv_w, p['conv_dw_w'][i], p['conv_dw_b'][i], p['conv_ln_g'][i], p['conv_ln_b'][i])
            x0, v = hyena_gate(z, 2 * conv_w, hy_w, p['hy_short_w'][i], p['hy_short_b'][i])
            kern, norm = hyena_filter(l, hy_w, p['hy_filt_w1'][i], p['hy_filt_b1'][i], p['hy_filt_w2'][i],
                                      p['hy_filt_b2'][i], p['hy_filt_w3'][i], p['hy_filt_b3'][i],
                                      p['hy_filt_freq'][i], p['hy_filt_w4'][i])
            kspec = filter_spectrum(kern, norm, hn1, hn2)
            dd = hyena_long_conv(v, x0, kspec, p['hy_bias'][i], hn1, hn2)
            parts = [c.reshape(m, conv_w), dd.reshape(m, hy_w)]
            xf, xb, r = out_proj(parts, wb['cd_w_out'], i, xf)
        hid = swiglu_in(xb, r, wb['w_gate'], wb['w_up'], layer)
        kh = hid.shape[1] // 2
        assert kh % LANES == 0
        xf = out_proj([(hid, 0, kh)], wb['w_down'], layer, xf, emit_norm=False)
        xf, xb, r = out_proj([(hid, 1, kh)], wb['w_down'], layer, xf, k0=kh)
    return rms_norm(xf, p['final_norm'], F32).reshape(b, l, d)


def kernel(x_prompt, x_sample, mix_norm, ffn_norm, final_norm, w_gate, w_up, w_down, ab_w_in, ab_w_out, attn_sink, cd_w_in, cd_w_out, conv_dw_w, conv_dw_b, conv_ln_g, conv_ln_b, hy_short_w, hy_short_b, hy_filt_w1, hy_filt_b1, hy_filt_w2, hy_filt_b2, hy_filt_w3, hy_filt_b3, hy_filt_freq, hy_filt_w4, hy_bias):
    p = dict(mix_norm=mix_norm, ffn_norm=ffn_norm, final_norm=final_norm, attn_sink=attn_sink,
             conv_dw_w=conv_dw_w, conv_dw_b=conv_dw_b, conv_ln_g=conv_ln_g, conv_ln_b=conv_ln_b,
             hy_short_w=hy_short_w, hy_short_b=hy_short_b, hy_filt_w1=hy_filt_w1, hy_filt_b1=hy_filt_b1,
             hy_filt_w2=hy_filt_w2, hy_filt_b2=hy_filt_b2, hy_filt_w3=hy_filt_w3, hy_filt_b3=hy_filt_b3,
             hy_filt_freq=hy_filt_freq, hy_filt_w4=hy_filt_w4, hy_bias=hy_bias)
    g_ab, g_cd, g_ffn = mix_norm[0::2, :, None], mix_norm[1::2, :, None], ffn_norm[:, :, None]
    wb = dict(w_gate=(g_ffn * w_gate).astype(BF16), w_up=(g_ffn * w_up).astype(BF16), w_down=w_down.astype(BF16),
              ab_w_in=(g_ab * ab_w_in).astype(BF16), ab_w_out=ab_w_out.astype(BF16),
              cd_w_in=(g_cd * cd_w_in).astype(BF16), cd_w_out=cd_w_out.astype(BF16))
    return (_trunk(x_prompt, p, wb), _trunk(x_sample, p, wb))
```
